```python
import math
import jax, jax.numpy as jnp
from jax import lax
import numpy as np

D_MODEL = 1024
BATCH = 32
SEQ = 256
DEPTH = 4
DEC_BATCH = 4
DEC_SEQ = 1024
PAST_LEN = 256

GRID_W = 64
MIX = D_MODEL
H_A = 4
DV_A = MIX // 2 // H_A
DK_A = DV_A // 2
GATE_RANK = 16
GATE_NORMALIZER = 16.0
H_B = 4
DK_B = MIX // 2 // H_B
DV_B = DK_B
CONV_K = 3
CHUNK = 64
N_EXPERTS = 32
TOP_K = 4
D_FF = D_MODEL
SWIGLU_LIMIT = 7.0
SWIGLU_ALPHA = 1.702
MOE_BLOCK = 128
N_MOD = 6
EPS = 1e-6
IN_SIZES = (H_A * DK_A, H_A * DK_A, H_A * DV_A, GATE_RANK, H_A * DV_A,
            2 * H_B * DK_B + H_B * DV_B, 2 * H_B, 2 * H_B, H_B * DV_B)
IN_DIM = sum(IN_SIZES)

kernel_name = 'hybrid_gla_gdn_moe_diffusion_step'


def rmsnorm(x, g):
    xf = x.astype(jnp.float32)
    y = xf * lax.rsqrt(jnp.mean(xf * xf, axis=-1, keepdims=True) + EPS)
    return (y * g.astype(jnp.float32)).astype(x.dtype)


def l2norm(x):
    return x * lax.rsqrt(jnp.sum(x * x, axis=-1, keepdims=True) + EPS)


def modulation(cond, w, b):
    m = jnp.matmul(jax.nn.silu(cond), w) + b
    return jnp.split(m, N_MOD, axis=-1)


def modulate(h, shift, scale):
    return h * (1 + scale) + shift


def flip(x):
    return jnp.flip(x, axis=1)


def grid_swap(x, r1, r2):
    b, n, d = x.shape
    return x.reshape(b, r1, r2, d).swapaxes(1, 2).reshape(b, n, d)


def to_chunks(x):
    b, l = x.shape[:2]
    return jnp.moveaxis(x.reshape(b, l // CHUNK, CHUNK, *x.shape[2:]), 1, 0)


def from_chunks(x):
    n, b, c = x.shape[:3]
    return jnp.moveaxis(x, 0, 1).reshape(b, n * c, *x.shape[3:])


def short_conv(x, w):
    ch = x.shape[-1]
    return lax.conv_general_dilated(
        x, w.astype(x.dtype)[:, None, :], window_strides=(1,),
        padding=[(CONV_K // 2, CONV_K // 2)],
        dimension_numbers=('NWC', 'WIO', 'NWC'), feature_group_count=ch)


def gla_chunked(q, k, v, g, s0):
    mask = jnp.tril(jnp.ones((CHUNK, CHUNK), bool))[None, :, :, None, None]

    def step(s, inp):
        qc, kc, vc, gc = inp
        b = jnp.cumsum(gc, axis=1)
        b_last = b[:, -1]
        diff = b[:, :, None] - b[:, None, :]
        dec = jnp.exp(jnp.where(mask, diff, -jnp.inf))
        attn = jnp.sum(qc[:, :, None] * kc[:, None] * dec, axis=-1)
        o = (jnp.einsum('bihd,bhdv->bihv', qc * jnp.exp(b), s)
             + jnp.einsum('bijh,bjhv->bihv', attn, vc))
        kd = kc * jnp.exp(b_last[:, None] - b)
        s = jnp.exp(b_last)[..., None] * s + jnp.einsum('bjhd,bjhv->bhdv', kd, vc)
        return s, o

    s, o = lax.scan(step, s0, (to_chunks(q), to_chunks(k), to_chunks(v), to_chunks(g)))
    return from_chunks(o), s


def gdn_chunked(q, k, v, g, beta, s0):
    tril = jnp.tril(jnp.ones((CHUNK, CHUNK), bool))
    strict = jnp.tril(jnp.ones((CHUNK, CHUNK), bool), -1)
    eye = jnp.eye(CHUNK, dtype=jnp.float32)

    def step(s, inp):
        qc, kc, vc, gc, bc = inp
        gam = jnp.cumsum(gc, axis=1)
        gam_h = jnp.moveaxis(gam, 1, -1)
        dec = jnp.exp(jnp.where(tril, gam_h[..., :, None] - gam_h[..., None, :], -jnp.inf))
        kk = jnp.einsum('bihd,bjhd->bhij', kc, kc)
        bh = jnp.moveaxis(bc, 1, -1)
        m = jnp.where(strict, bh[..., :, None] * kk * dec, 0.0)
        kbs = jnp.einsum('bihd,bhdv->bhiv', kc * (bc * jnp.exp(gam))[..., None], s)
        rhs = jnp.moveaxis(vc * bc[..., None], 1, 2) - kbs
        u = lax.linalg.triangular_solve(eye + m, rhs, left_side=True, lower=True)
        qk = jnp.einsum('bihd,bjhd->bhij', qc, kc) * dec
        o = (jnp.einsum('bihd,bhdv->bhiv', qc * jnp.exp(gam)[..., None], s)
             + jnp.einsum('bhij,bhjv->bhiv', qk, u))
        g_last = gam[:, -1]
        kd = kc * jnp.exp(g_last[:, None] - gam)[..., None]
        s = jnp.exp(g_last)[..., None, None] * s + jnp.einsum('bjhd,bhjv->bhdv', kd, u)
        return s, jnp.moveaxis(o, 1, 2)

    s, o = lax.scan(step, s0, (to_chunks(q), to_chunks(k), to_chunks(v), to_chunks(g), to_chunks(beta)))
    return from_chunks(o), s


def token_mixers(h, st_a, st_b, w_in, w_gate2, b_gate, g_onorm_a, conv_w, a_log, dt_bias, g_onorm_b, w_out):
    bsz, L, _ = h.shape
    p = jnp.einsum('bld,de->ble', h, w_in).astype(jnp.float32)
    splits = np.cumsum(IN_SIZES)[:-1].tolist()
    qa, ka, va, glr, oga, qkv_b, ab, bb, zb = jnp.split(p, splits, axis=-1)
    st_a = st_a.astype(jnp.float32)
    st_b = st_b.astype(jnp.float32)

    qa = qa.reshape(bsz, L, H_A, DK_A) * DK_A ** -0.5
    ka = ka.reshape(bsz, L, H_A, DK_A)
    va = va.reshape(bsz, L, H_A, DV_A)
    gk = jax.nn.log_sigmoid(jnp.einsum('blr,nrk->blnk', glr, w_gate2) + b_gate) / GATE_NORMALIZER
    gk = gk.reshape(bsz, L, 2, H_A, DK_A)
    oa_f, sa_f = gla_chunked(qa, ka, va, gk[:, :, 0], st_a[:, 0])
    oa_b, sa_b = gla_chunked(flip(qa), flip(ka), flip(va), flip(gk[:, :, 1]), st_a[:, 1])
    oa = rmsnorm(oa_f + flip(oa_b), g_onorm_a) * jax.nn.silu(oga.reshape(bsz, L, H_A, DV_A))

    qkv = jax.nn.silu(short_conv(qkv_b, conv_w))
    qb, kb, vb = jnp.split(qkv, [H_B * DK_B, 2 * H_B * DK_B], axis=-1)
    qb = l2norm(qb.reshape(bsz, L, H_B, DK_B)) * DK_B ** -0.5
    kb = l2norm(kb.reshape(bsz, L, H_B, DK_B))
    vb = vb.reshape(bsz, L, H_B, DV_B)
    gb = -jnp.exp(a_log.astype(jnp.float32)) * jax.nn.softplus(ab.reshape(bsz, L, 2, H_B) + dt_bias)
    beta = jax.nn.sigmoid(bb.reshape(bsz, L, 2, H_B))
    ob_f, sb_f = gdn_chunked(qb, kb, vb, gb[:, :, 0], beta[:, :, 0], st_b[:, 0])
    ob_b, sb_b = gdn_chunked(flip(qb), flip(kb), flip(vb), flip(gb[:, :, 1]), flip(beta[:, :, 1]), st_b[:, 1])
    ob = rmsnorm(ob_f + flip(ob_b), g_onorm_b) * jax.nn.silu(zb.reshape(bsz, L, H_B, DV_B))

    o = jnp.concatenate([oa.reshape(bsz, L, -1), ob.reshape(bsz, L, -1)], axis=-1)
    y = jnp.einsum('blm,md->bld', o, w_out).astype(h.dtype)
    return y, jnp.stack([sa_f, sa_b], axis=1), jnp.stack([sb_f, sb_b], axis=1)


def moe(x, router_w, router_b, w_gu, b_gu, w_down, b_down):
    t, d = x.shape
    logits = (jnp.matmul(x, router_w) + router_b).astype(jnp.float32)
    top_v, top_i = lax.top_k(logits, TOP_K)
    gates = jax.nn.softmax(top_v, axis=-1)
    tk = t * TOP_K
    e_flat = top_i.reshape(-1)
    tok_flat = jnp.repeat(jnp.arange(t, dtype=jnp.int32), TOP_K)
    w_flat = gates.reshape(-1)
    order = jnp.argsort(e_flat)
    e_sorted = e_flat[order]
    sizes = jnp.bincount(e_flat, length=N_EXPERTS)
    padded = (sizes + MOE_BLOCK - 1) // MOE_BLOCK * MOE_BLOCK
    start = jnp.cumsum(sizes) - sizes
    pend = jnp.cumsum(padded)
    pstart = pend - padded
    dest = pstart[e_sorted] + jnp.arange(tk, dtype=jnp.int32) - start[e_sorted]
    n_blocks = -(-tk // MOE_BLOCK) + N_EXPERTS
    cap = n_blocks * MOE_BLOCK
    buf_tok = jnp.zeros((cap,), jnp.int32).at[dest].set(tok_flat[order])
    buf_w = jnp.zeros((cap,), jnp.float32).at[dest].set(w_flat[order])
    blk_e = jnp.minimum(jnp.searchsorted(pend, jnp.arange(n_blocks) * MOE_BLOCK, side='right'), N_EXPERTS - 1)
    xb = x[buf_tok].reshape(n_blocks, MOE_BLOCK, d)

    def expert_block(args):
        xe, e = args
        gu = jnp.matmul(xe, w_gu[e]) + b_gu[e]
        gate = jnp.minimum(gu[..., 0::2], SWIGLU_LIMIT)
        up = jnp.clip(gu[..., 1::2], -SWIGLU_LIMIT, SWIGLU_LIMIT)
        hdn = (up + 1) * (gate * jax.nn.sigmoid(SWIGLU_ALPHA * gate))
        return jnp.matmul(hdn, w_down[e]) + b_down[e]

    yb = lax.map(expert_block, (xb, blk_e)).reshape(cap, d)
    y = jnp.zeros((t, d), jnp.float32).at[buf_tok].add(yb.astype(jnp.float32) * buf_w[:, None])
    return y.astype(x.dtype)


def setup_inputs(seed: int = 0) -> dict:
    key = jax.random.key(seed)
    ks = jax.random.split(key, 28)

    def nrm(k, shape, s):
        return jax.random.normal(k, shape, jnp.float32) * s

    qa_w = H_A * DK_A
    dt = jnp.exp(jax.random.uniform(ks[13], (DEPTH, 2, H_B), jnp.float32, math.log(1e-3), math.log(1e-1)))
    return {
        'x_prompt': nrm(ks[0], (BATCH, SEQ, D_MODEL), 1.0),
        'x_sample': nrm(ks[1], (DEC_BATCH, DEC_SEQ, D_MODEL), 1.0),
        'state_gla': nrm(ks[2], (DEC_BATCH, DEPTH, 2, H_A, DK_A, DV_A), 0.5),
        'state_gdn': nrm(ks[3], (DEC_BATCH, DEPTH, 2, H_B, DK_B, DV_B), 0.2),
        'c': nrm(ks[4], (DEC_BATCH, D_MODEL), 1.0),
        'c_ctx': nrm(ks[5], (D_MODEL,), 1.0),
        'w_mod': nrm(ks[6], (DEPTH, D_MODEL, N_MOD * D_MODEL), 0.5 * D_MODEL ** -0.5),
        'b_mod': nrm(ks[7], (DEPTH, N_MOD * D_MODEL), 0.02),
        'g_norm1': 1.0 + nrm(ks[8], (DEPTH, D_MODEL), 0.02),
        'g_norm2': 1.0 + nrm(ks[9], (DEPTH, D_MODEL), 0.02),
        'w_in': nrm(ks[10], (DEPTH, D_MODEL, IN_DIM), D_MODEL ** -0.5),
        'gla_w_gate2': nrm(ks[11], (DEPTH, 2, GATE_RANK, qa_w), GATE_RANK ** -0.5),
        'gla_b_gate': nrm(ks[12], (DEPTH, 2, qa_w), 0.1),
        'gla_g_onorm': 1.0 + nrm(ks[14], (DEPTH, DV_A), 0.02),
        'gdn_conv_w': nrm(ks[15], (DEPTH, CONV_K, 2 * H_B * DK_B + H_B * DV_B), CONV_K ** -0.5),
        'gdn_a_log': jnp.log(jax.random.uniform(ks[16], (DEPTH, 2, H_B), jnp.float32, 1.0, 16.0)),
        'gdn_dt_bias': dt + jnp.log(-jnp.expm1(-dt)),
        'gdn_g_onorm': 1.0 + nrm(ks[17], (DEPTH, DV_B), 0.02),
        'w_out': nrm(ks[18], (DEPTH, MIX, D_MODEL), MIX ** -0.5),
        'router_w': nrm(ks[19], (DEPTH, D_MODEL, N_EXPERTS), D_MODEL ** -0.5),
        'router_b': nrm(ks[20], (DEPTH, N_EXPERTS), 0.01),
        'w_gu': nrm(ks[21], (DEPTH, N_EXPERTS, D_MODEL, 2 * D_FF), D_MODEL ** -0.5),
        'b_gu': nrm(ks[22], (DEPTH, N_EXPERTS, 2 * D_FF), 0.02),
        'w_down': nrm(ks[23], (DEPTH, N_EXPERTS, D_FF, D_MODEL), D_FF ** -0.5),
        'b_down': nrm(ks[24], (DEPTH, N_EXPERTS, D_MODEL), 0.02),
        'g_final': 1.0 + nrm(ks[25], (D_MODEL,), 0.02),
    }


def reference(x_prompt, x_sample, state_gla, state_gdn, c, c_ctx, w_mod, b_mod, g_norm1, g_norm2, w_in,
              gla_w_gate2, gla_b_gate, gla_g_onorm, gdn_conv_w, gdn_a_log, gdn_dt_bias, gdn_g_onorm, w_out,
              router_w, router_b, w_gu, b_gu, w_down, b_down, g_final):
    bp, sp, dm = x_prompt.shape
    bs, ns, _ = x_sample.shape
    rows = ns // GRID_W
    zero_a = jnp.zeros((bp, 2, H_A, DK_A, DV_A), jnp.float32)
    zero_b = jnp.zeros((bp, 2, H_B, DK_B, DV_B), jnp.float32)
    xp, xs = x_prompt, x_sample
    new_gla, new_gdn = [], []
    for l in range(DEPTH):
        sh1p, sc1p, ga1p, sh2p, sc2p, ga2p = modulation(c_ctx, w_mod[l], b_mod[l])
        sh1s, sc1s, ga1s, sh2s, sc2s, ga2s = [m[:, None, :] for m in modulation(c, w_mod[l], b_mod[l])]

        def mix(h, s_a, s_b):
            return token_mixers(h, s_a, s_b, w_in[l], gla_w_gate2[l], gla_b_gate[l], gla_g_onorm[l],
                                gdn_conv_w[l], gdn_a_log[l], gdn_dt_bias[l], gdn_g_onorm[l], w_out[l])

        hp = modulate(rmsnorm(xp, g_norm1[l]), sh1p, sc1p)
        mp, sa, sb = mix(hp, zero_a, zero_b)
        hs = modulate(rmsnorm(xs, g_norm1[l]), sh1s, sc1s)
        if l % 2 == 1:
            hs = grid_swap(hs, rows, GRID_W)
        ms, _, _ = mix(hs, state_gla[:, l], state_gdn[:, l])
        if l % 2 == 1:
            ms = grid_swap(ms, GRID_W, rows)
        xp = xp + ga1p * mp
        xs = xs + ga1s * ms
        new_gla.append(sa)
        new_gdn.append(sb)

        hp2 = modulate(rmsnorm(xp, g_norm2[l]), sh2p, sc2p)
        hs2 = modulate(rmsnorm(xs, g_norm2[l]), sh2s, sc2s)
        f = moe(jnp.concatenate([hp2.reshape(bp * sp, dm), hs2.reshape(bs * ns, dm)], axis=0),
                router_w[l], router_b[l], w_gu[l], b_gu[l], w_down[l], b_down[l])
        xp = xp + ga2p * f[:bp * sp].reshape(bp, sp, dm)
        xs = xs + ga2s * f[bp * sp:].reshape(bs, ns, dm)

    y_prompt = rmsnorm(xp, g_final)
    y_sample = rmsnorm(xs, g_final)
    new_state_gla = jnp.stack(new_gla, axis=1)
    new_state_gdn = jnp.stack(new_gdn, axis=1)
    return (y_prompt, y_sample, new_state_gla, new_state_gdn)
```

```python
import functools

import numpy as np
import jax
import jax.numpy as jnp
from jax import lax
from jax.experimental import pallas as pl
from jax.experimental.pallas import tpu as pltpu

F32 = jnp.float32
BF16 = jnp.bfloat16

D_MODEL = 1024
GRID_W = 64
H_A, DK_A, DV_A = 4, 64, 128
H_B, DK_B, DV_B = 4, 128, 128
GATE_RANK = 16
GATE_NORMALIZER = 16.0
CHUNK = 64
N_EXPERTS = 32
TOP_K = 4
D_FF = 1024
SWIGLU_LIMIT = 7.0
SWIGLU_ALPHA = 1.702
N_MOD = 6
EPS = 1e-6

QA_W = H_A * DK_A
VA_W = H_A * DV_A
QB_W = H_B * DK_B
VB_W = H_B * DV_B
QKVB_W = 2 * QB_W + VB_W
PA_W = 2 * QA_W + VA_W
GZ_W = VA_W + VB_W
SM_W = 128
SM_A0 = GATE_RANK
SM_B0 = GATE_RANK + 2 * H_B
W_IN_COLS = PA_W + GZ_W + QKVB_W + SM_W

TOKEN_TILE = 256
EXPERT_BLOCK = 256
LANES = 128
VMEM_LIMIT_BYTES = 56 * 2**20

_NT = (((1,), (1,)), ((), ()))
_TN = (((0,), (0,)), ((), ()))


def _params(*sem):
    return pltpu.CompilerParams(dimension_semantics=sem, vmem_limit_bytes=VMEM_LIMIT_BYTES)


def _dot(a, b):
    return jnp.dot(a.astype(BF16), b.astype(BF16), preferred_element_type=F32)


def _dot_nt(a, b):
    return lax.dot_general(a.astype(BF16), b.astype(BF16), _NT, preferred_element_type=F32)


def _dot_tn(a, b):
    return lax.dot_general(a.astype(BF16), b.astype(BF16), _TN, preferred_element_type=F32)


def _split(x):
    hi = x.astype(BF16)
    lo = (x - hi.astype(F32)).astype(BF16)
    return hi, lo


def _dot_x2(m, x):
    hi, lo = _split(x)
    return _dot(m, hi) + _dot(m, lo)


def _dot_2x(x, m):
    hi, lo = _split(x)
    return _dot(hi, m) + _dot(lo, m)


def _dot3(a, b):
    ah, al = _split(a)
    bh, bl = _split(b)
    return _dot(ah, bh) + _dot(ah, bl) + _dot(al, bh)


def _sigmoid(x):
    return 1.0 / (1.0 + jnp.exp(-x))


def _softplus(x):
    return jnp.maximum(x, 0.0) + jnp.log(1.0 + jnp.exp(-jnp.abs(x)))


def _rms(x):
    return x * lax.rsqrt(jnp.mean(x * x, axis=-1, keepdims=True) + EPS)


def _stack4(x, mask):
    return jnp.concatenate([x, x, x, x], axis=0) * mask


def _block_mask(rows, row_blk, cols, col_blk):
    r = np.arange(rows)[:, None] // row_blk
    c = np.arange(cols)[None, :] // col_blk
    return (r == c).astype(np.float32)


@functools.lru_cache(maxsize=None)
def _gla_consts():
    c = CHUNK
    w = np.zeros((7, c, c), np.float32)
    m = np.zeros((7, c, c), np.float32)
    w[0] = np.tril(np.ones((c, c)))
    m[0] = np.eye(c)
    i = np.arange(c)
    for l, s in enumerate((32, 16, 8, 4, 2, 1), start=1):
        ref = (i // (2 * s)) * 2 * s + s - 1
        for r in range(c):
            if r > ref[r]:
                w[l, r, ref[r] + 1:r + 1] = 1.0
            else:
                w[l, r, r + 1:ref[r] + 1] = 1.0
        same = (i[:, None] // (2 * s)) == (i[None, :] // (2 * s))
        m[l] = same & ((i[:, None] % (2 * s)) >= s) & ((i[None, :] % (2 * s)) < s)
    assert np.array_equal(m.sum(0), np.tril(np.ones((c, c))))
    wall = np.stack([w.reshape(7 * c, c), w[:, ::-1, ::-1].reshape(7 * c, c)])
    masks = np.stack([np.tile(m, (1, 1, H_A)), np.tile(m[:, ::-1, ::-1], (1, 1, H_A))])
    return dict(
        wall=jnp.asarray(wall, BF16),
        masks=jnp.asarray(masks, F32),
        hm=jnp.asarray(_block_mask(H_A * c, c, QA_W, DK_A), F32),
        vm=jnp.asarray(_block_mask(H_A * c, c, VA_W, DV_A), F32),
        bd=jnp.asarray(_block_mask(VA_W, DV_A, QA_W, DK_A), F32),
    )


@functools.lru_cache(maxsize=None)
def _gdn_consts():
    c = CHUNK
    tril = np.tril(np.ones((c, c), np.float32))
    lc = np.stack([tril, tril.T])
    ut = np.stack([np.tile(lc[d].T, (1, H_B)) for d in range(2)])
    cm = np.stack([np.tile(lc[d], (1, H_B)) for d in range(2)])
    eye = np.tile(np.eye(c, dtype=np.float32), (1, H_B))
    sm = cm - eye[None]
    i = np.arange(c)
    dblk = np.tile(((i[:, None] // 16) == (i[None, :] // 16)).astype(np.float32), (1, H_B))
    eg = np.zeros((2, SM_W, H_B * c), np.float32)
    eb = np.zeros((2, SM_W, H_B * c), np.float32)
    egw = np.zeros((2, SM_W, QB_W), np.float32)
    ebw = np.zeros((2, SM_W, QB_W), np.float32)
    for d in range(2):
        for h in range(H_B):
            eg[d, SM_A0 + d * H_B + h, h * c:(h + 1) * c] = 1.0
            eb[d, SM_B0 + d * H_B + h, h * c:(h + 1) * c] = 1.0
            egw[d, SM_A0 + d * H_B + h, h * DK_B:(h + 1) * DK_B] = 1.0
            ebw[d, SM_B0 + d * H_B + h, h * DK_B:(h + 1) * DK_B] = 1.0
    return dict(
        lc=jnp.asarray(lc, BF16), ut=jnp.asarray(ut, F32), cm=jnp.asarray(cm, F32), sm=jnp.asarray(sm, F32),
        eye=jnp.asarray(eye, F32), dblk=jnp.asarray(dblk, F32),
        eg=jnp.asarray(eg, BF16), eb=jnp.asarray(eb, BF16), egw=jnp.asarray(egw, BF16), ebw=jnp.asarray(ebw, BF16),
        hm=jnp.asarray(_block_mask(H_B * c, c, H_B * c, c), F32),
        km=jnp.asarray(_block_mask(H_B * c, c, QB_W, DK_B), F32),
    )


@functools.lru_cache(maxsize=None)
def _deinterleave_matrix():
    p = np.zeros((2 * LANES, 2 * LANES), np.float32)
    j = np.arange(LANES)
    p[2 * j, j] = 1.0
    p[2 * j + 1, LANES + j] = 1.0
    return jnp.asarray(p, BF16)


def _mod_kernel(c_ref, w_ref, b_ref, o_ref):
    c = c_ref[...]
    o_ref[0] = _dot3(c * _sigmoid(c), w_ref[0]) + b_ref[0]


def _modulation(cond, w_mod, b_mod):
    depth, d, n = w_mod.shape
    tn = 1536
    return pl.pallas_call(
        _mod_kernel,
        grid=(depth, n // tn),
        in_specs=[pl.BlockSpec((8, d), lambda l, j: (0, 0)),
                  pl.BlockSpec((1, d, tn), lambda l, j: (l, 0, j)),
                  pl.BlockSpec((1, 1, tn), lambda l, j: (l, 0, j))],
        out_specs=pl.BlockSpec((1, 8, tn), lambda l, j: (l, 0, j)),
        out_shape=jax.ShapeDtypeStruct((depth, 8, n), F32),
        compiler_params=_params("parallel", "parallel"),
        name="modulation",
    )(cond, w_mod, b_mod.reshape(depth, 1, n))


def _pre_kernel(x_ref, mod_ref, g_ref, w_ref, pa_ref, gz_ref, pb_ref, sm_ref):
    h = _rms(x_ref[...]) * g_ref[...]
    h = h * (1.0 + mod_ref[0, 1:2, :]) + mod_ref[0, 0:1, :]
    p = jnp.dot(h.astype(BF16), w_ref[...], preferred_element_type=F32)
    pa_ref[...] = p[:, :PA_W]
    gz_ref[...] = p[:, PA_W:PA_W + GZ_W]
    pb_ref[...] = p[:, PA_W + GZ_W:PA_W + GZ_W + QKVB_W]
    sm_ref[...] = p[:, PA_W + GZ_W + QKVB_W:]


def _pre_mixer(x, modtab, g1, w_in_r, group_of_tile):
    t, d = x.shape
    tm = TOKEN_TILE
    row = lambda i: (i, 0)
    return pl.pallas_call(
        _pre_kernel,
        grid=(t // tm,),
        in_specs=[pl.BlockSpec((tm, d), row),
                  pl.BlockSpec((1, 8, d), lambda i: (group_of_tile(i), 0, 0)),
                  pl.BlockSpec((1, d), lambda i: (0, 0)),
                  pl.BlockSpec((d, W_IN_COLS), lambda i: (0, 0))],
        out_specs=[pl.BlockSpec((tm, PA_W), row), pl.BlockSpec((tm, GZ_W), row),
                   pl.BlockSpec((tm, QKVB_W), row), pl.BlockSpec((tm, SM_W), row)],
        out_shape=[jax.ShapeDtypeStruct((t, PA_W), F32), jax.ShapeDtypeStruct((t, GZ_W), F32),
                   jax.ShapeDtypeStruct((t, QKVB_W), F32), jax.ShapeDtypeStruct((t, SM_W), F32)],
        compiler_params=_params("parallel"),
        name="pre_mixer",
    )(x, modtab, g1.reshape(1, d), w_in_r)


def _gla_kernel(pa_ref, sm_ref, wall_ref, mask_ref, wg_ref, bg_ref, hm_ref, vm_ref, bd_ref, st0_ref,
                o_ref, stf_ref, st_scr, *, nchunk):
    c = CHUNK
    hm = hm_ref[...]
    vm = vm_ref[...]
    bd = bd_ref[...]
    for d in range(2):
        st_scr[...] = st0_ref[0, d]
        wall = wall_ref[d]
        wg = wg_ref[d]
        bg = bg_ref[d]

        def body(ci, carry, d=d, wall=wall, wg=wg, bg=bg):
            cc = ci if d == 0 else nchunk - 1 - ci
            rows = pl.ds(pl.multiple_of(cc * c, c), c)
            q = pa_ref[rows, 0:QA_W] * (DK_A ** -0.5)
            k = pa_ref[rows, QA_W:2 * QA_W]
            v = pa_ref[rows, 2 * QA_W:PA_W]
            x = _dot3(sm_ref[rows, :], wg) + bg
            gk = (jnp.minimum(x, 0.0) - jnp.log(1.0 + jnp.exp(-jnp.abs(x)))) * (1.0 / GATE_NORMALIZER)
            e_all = _dot_x2(wall, gk)
            b = e_all[0:c]
            a = None
            for l in range(7):
                if l == 0:
                    qt, kt = q, k
                else:
                    f = jnp.exp(e_all[l * c:(l + 1) * c])
                    qt, kt = q * f, k * f
                r = _dot_nt(qt, _stack4(kt, hm)) * mask_ref[d, l]
                a = r if a is None else a + r
            st = st_scr[...]
            o = _dot(a, _stack4(v, vm)) + _dot_nt(q * jnp.exp(b), st)
            if d == 0:
                o_ref[rows, :] = o
            else:
                o_ref[rows, :] = o_ref[rows, :] + o
            b_last = b[c - 1:c] if d == 0 else b[0:1]
            kd = k * jnp.exp(b_last - b)
            st_scr[...] = st * jnp.exp(b_last) + _dot_tn(v, kd) * bd
            return carry

        lax.fori_loop(0, nchunk, body, 0)
        stf_ref[0, d] = st_scr[...]


def _gla(pa, sm, st0, wg, bg, *, nseq, seqlen, row_block0, shared_state):
    cst = _gla_consts()
    nchunk = seqlen // CHUNK
    seq = lambda b: (row_block0 + b, 0)
    full = lambda *shape: pl.BlockSpec(shape, lambda b: (0,) * len(shape))
    st_in = (lambda b: (0, 0, 0, 0)) if shared_state else (lambda b: (b, 0, 0, 0))
    return pl.pallas_call(
        functools.partial(_gla_kernel, nchunk=nchunk),
        grid=(nseq,),
        in_specs=[pl.BlockSpec((seqlen, PA_W), seq), pl.BlockSpec((seqlen, SM_W), seq),
                  full(2, 7 * CHUNK, CHUNK), full(2, 7, CHUNK, QA_W), full(2, SM_W, QA_W), full(2, 1, QA_W),
                  full(QA_W, QA_W), full(QA_W, VA_W), full(VA_W, QA_W),
                  pl.BlockSpec((1, 2, VA_W, QA_W), st_in)],
        out_specs=[pl.BlockSpec((seqlen, VA_W), lambda b: (b, 0)),
                   pl.BlockSpec((1, 2, VA_W, QA_W), lambda b: (b, 0, 0, 0))],
        out_shape=[jax.ShapeDtypeStruct((nseq * seqlen, VA_W), F32),
                   jax.ShapeDtypeStruct((nseq, 2, VA_W, QA_W), F32)],
        scratch_shapes=[pltpu.VMEM((VA_W, QA_W), F32)],
        compiler_params=_params("parallel"),
        name="gla_l%d" % seqlen,
    )(pa, sm, cst["wall"], cst["masks"], wg, bg, cst["hm"], cst["vm"], cst["bd"], st0)


def _mm_heads(x, y, hm):
    xh, xl = _split(x)
    yh, yl = _split(_stack4(y, hm))
    return _dot(xh, yh) + _dot(xh, yl) + _dot(xl, yh)


def _unit_tri_inverse(m, eye, dblk, hm):
    mm = functools.partial(_mm_heads, hm=hm)
    dg = m * dblk
    lo = m - dg
    d2 = mm(dg, dg)
    d3 = mm(dg, d2)
    d4 = mm(d2, d2)
    p1 = eye - dg + d2 - d3
    p2 = p1 + mm(p1, d4)
    d8 = mm(d4, d4)
    td = p2 + mm(p2, d8)
    e1 = mm(td, lo)
    e2 = mm(e1, e1)
    e3 = mm(e1, e2)
    return mm(eye - e1 + e2 - e3, td)


def _gdn_kernel(pb_ref, sm_ref, cw_ref, al_ref, dtb_ref, lc_ref, ut_ref, cm_ref, smk_ref, eye_ref, dblk_ref,
                eg_ref, eb_ref, egw_ref, ebw_ref, hm_ref, km_ref, st0_ref, o_ref, stf_ref, qkv_scr, s_scr,
                *, nchunk):
    c = CHUNK
    seqlen = nchunk * c
    w0 = cw_ref[0:1, :]
    w1 = cw_ref[1:2, :]
    w2 = cw_ref[2:3, :]

    def conv_body(cc, carry):
        r0 = pl.multiple_of(cc * c, c)
        x = pb_ref[pl.ds(r0, c), :]
        prow = pb_ref[pl.ds(jnp.maximum(r0 - 1, 0), 1), :]
        nrow = pb_ref[pl.ds(jnp.minimum(r0 + c, seqlen - 1), 1), :]
        prow = jnp.where(cc > 0, prow, 0.0)
        nrow = jnp.where(cc < nchunk - 1, nrow, 0.0)
        rid = lax.broadcasted_iota(jnp.int32, x.shape, 0)
        xp = jnp.where(rid == 0, prow, pltpu.roll(x, 1, axis=0))
        xn = jnp.where(rid == c - 1, nrow, pltpu.roll(x, c - 1, axis=0))
        y = w0 * xp + w1 * x + w2 * xn
        s = y * _sigmoid(y)
        parts = []
        for h in range(2 * H_B):
            sl = s[:, h * DK_B:(h + 1) * DK_B]
            nrm = lax.rsqrt(jnp.sum(sl * sl, axis=-1, keepdims=True) + EPS)
            parts.append(sl * (nrm * (DK_B ** -0.5 if h < H_B else 1.0)))
        parts.append(s[:, 2 * QB_W:])
        qkv_scr[pl.ds(r0, c), :] = jnp.concatenate(parts, axis=1)
        return carry

    lax.fori_loop(0, nchunk, conv_body, 0)

    hm = hm_ref[...]
    km = km_ref[...]
    eye = eye_ref[...]
    dblk = dblk_ref[...]
    ones = jnp.ones((c, c), BF16)
    neg_a = -jnp.exp(al_ref[...])
    dtb = dtb_ref[...]
    for d in range(2):
        s_scr[...] = st0_ref[0, d]
        lc = lc_ref[d]

        def body(ci, carry, d=d, lc=lc):
            cc = ci if d == 0 else nchunk - 1 - ci
            rows = pl.ds(pl.multiple_of(cc * c, c), c)
            q = qkv_scr[rows, 0:QB_W]
            k = qkv_scr[rows, QB_W:2 * QB_W]
            v = qkv_scr[rows, 2 * QB_W:QKVB_W]
            sm = sm_ref[rows, :]
            g_all = neg_a * _softplus(sm + dtb)
            beta_all = _sigmoid(sm)
            g_b = _dot_2x(g_all, eg_ref[d])
            gam_i = _dot_x2(lc, g_b)
            gam_j = _dot_x2(ones, g_b * ut_ref[d])
            beta_i = _dot_2x(beta_all, eb_ref[d])
            gam_w = _dot_x2(lc, _dot_2x(g_all, egw_ref[d]))
            beta_w = _dot_2x(beta_all, ebw_ref[d])
            dec = jnp.exp(jnp.minimum(gam_i - gam_j, 0.0)) * cm_ref[d]
            kkqk = _dot_nt(jnp.concatenate([k, q], axis=0), _stack4(k, km))
            m = smk_ref[d] * beta_i * kkqk[:c] * dec
            t_inv = _unit_tri_inverse(m, eye, dblk, hm)
            eg_w = jnp.exp(gam_w)
            kh = k * (beta_w * eg_w)
            qh = q * eg_w
            kbs, qs = [], []
            for h in range(H_B):
                hs = slice(h * DK_B, (h + 1) * DK_B)
                res = _dot(jnp.concatenate([kh[:, hs], qh[:, hs]], axis=0), s_scr[h])
                kbs.append(res[:c])
                qs.append(res[c:])
            rhs = v * beta_w - jnp.concatenate(kbs, axis=1)
            u = _dot(t_inv, _stack4(rhs, km))
            o = jnp.concatenate(qs, axis=1) + _dot(kkqk[c:] * dec, _stack4(u, km))
            if d == 0:
                o_ref[rows, :] = o
            else:
                o_ref[rows, :] = o_ref[rows, :] + o
            g_last = gam_w[c - 1:c] if d == 0 else gam_w[0:1]
            kd = k * jnp.exp(g_last - gam_w)
            for h in range(H_B):
                hs = slice(h * DK_B, (h + 1) * DK_B)
                s_scr[h] = s_scr[h] * jnp.exp(g_last[:, hs]) + _dot_tn(kd[:, hs], u[:, hs])
            return carry

        lax.fori_loop(0, nchunk, body, 0)
        stf_ref[0, d] = s_scr[...]


def _gdn(pb, sm, st0, conv_w, a_log_vec, dtb_vec, *, nseq, seqlen, row_block0, shared_state):
    cst = _gdn_consts()
    nchunk = seqlen // CHUNK
    hc = H_B * CHUNK
    seq = lambda b: (row_block0 + b, 0)
    full = lambda *shape: pl.BlockSpec(shape, lambda b: (0,) * len(shape))
    st_in = (lambda b: (0, 0, 0, 0, 0)) if shared_state else (lambda b: (b, 0, 0, 0, 0))
    return pl.pallas_call(
        functools.partial(_gdn_kernel, nchunk=nchunk),
        grid=(nseq,),
        in_specs=[pl.BlockSpec((seqlen, QKVB_W), seq), pl.BlockSpec((seqlen, SM_W), seq),
                  full(8, QKVB_W), full(1, SM_W), full(1, SM_W),
                  full(2, CHUNK, CHUNK), full(2, CHUNK, hc), full(2, CHUNK, hc), full(2, CHUNK, hc),
                  full(CHUNK, hc), full(CHUNK, hc),
                  full(2, SM_W, hc), full(2, SM_W, hc), full(2, SM_W, QB_W), full(2, SM_W, QB_W),
                  full(hc, hc), full(hc, QB_W),
                  pl.BlockSpec((1, 2, H_B, DK_B, DV_B), st_in)],
        out_specs=[pl.BlockSpec((seqlen, VB_W), lambda b: (b, 0)),
                   pl.BlockSpec((1, 2, H_B, DK_B, DV_B), lambda b: (b, 0, 0, 0, 0))],
        out_shape=[jax.ShapeDtypeStruct((nseq * seqlen, VB_W), F32),
                   jax.ShapeDtypeStruct((nseq, 2, H_B, DK_B, DV_B), F32)],
        scratch_shapes=[pltpu.VMEM((seqlen, QKVB_W), F32), pltpu.VMEM((H_B, DK_B, DV_B), F32)],
        compiler_params=_params("parallel"),
        name="gdn_l%d" % seqlen,
    )(pb, sm, conv_w, a_log_vec, dtb_vec, cst["lc"], cst["ut"], cst["cm"], cst["sm"], cst["eye"], cst["dblk"],
      cst["eg"], cst["eb"], cst["egw"], cst["ebw"], cst["hm"], cst["km"], st0)


def _post_kernel(x_ref, oa_ref, ob_ref, gz_ref, mod_ref, gon_ref, wo_ref, g2_ref, rw_ref, rb_ref, ls_ref,
                 x1_ref, h2_ref, route_ref, cnt_ref):
    tm = x_ref.shape[0]

    @pl.when(pl.program_id(0) == 0)
    def _():
        cnt_ref[...] = jnp.zeros_like(cnt_ref)

    gz = gz_ref[...]
    heads = []
    for grp, o_ref in enumerate((oa_ref, ob_ref)):
        o = o_ref[...]
        for h in range(H_A):
            hs = slice(h * DV_A, (h + 1) * DV_A)
            gate = gz[:, grp * VA_W + h * DV_A:grp * VA_W + (h + 1) * DV_A]
            heads.append(_rms(o[:, hs]) * gon_ref[grp:grp + 1, :] * (gate * _sigmoid(gate)))
    mixed = jnp.concatenate(heads, axis=1)
    y = jnp.dot(mixed.astype(BF16), wo_ref[...], preferred_element_type=F32)
    x1 = x_ref[...] + mod_ref[0, 2:3, :] * y
    x1_ref[...] = x1
    h2 = _rms(x1) * g2_ref[...]
    h2 = h2 * (1.0 + mod_ref[0, 4:5, :]) + mod_ref[0, 3:4, :]
    h2_ref[...] = h2

    logits = _dot3(h2, rw_ref[...]) + rb_ref[...]
    lane = lax.broadcasted_iota(jnp.int32, logits.shape, 1).astype(F32)
    vals, idxs = [], []
    rem = logits
    for _ in range(TOP_K):
        mx = jnp.max(rem, axis=-1, keepdims=True)
        ix = jnp.min(jnp.where(rem == mx, lane, float(LANES)), axis=-1, keepdims=True)
        vals.append(mx)
        idxs.append(ix)
        rem = jnp.where(lane == ix, -jnp.inf, rem)
    exps = [jnp.exp(v - vals[0]) for v in vals]
    inv = 1.0 / (exps[0] + exps[1] + exps[2] + exps[3])
    onehot = jnp.zeros(logits.shape, F32)
    for ix in idxs:
        onehot = onehot + (lane == ix).astype(F32)
    before = _dot(ls_ref[...], onehot) + cnt_ref[...]
    route = jnp.zeros(logits.shape, F32)
    for kk in range(TOP_K):
        rank = jnp.sum(jnp.where(lane == idxs[kk], before, 0.0), axis=-1, keepdims=True)
        route = jnp.where(lane == kk, idxs[kk], route)
        route = jnp.where(lane == TOP_K + kk, exps[kk] * inv, route)
        route = jnp.where(lane == 2 * TOP_K + kk, rank, route)
    route_ref[...] = route
    cnt_ref[...] = cnt_ref[...] + jnp.sum(onehot, axis=0, keepdims=True)


def _post_mixer(x, oa, ob, gz, modtab, gon, w_out, g2, rw, rb, group_of_tile):
    t, d = x.shape
    tm = TOKEN_TILE
    row = lambda i: (i, 0)
    const = lambda i: (0, 0)
    lstrict = jnp.asarray(np.tril(np.ones((tm, tm), np.float32), -1), BF16)
    return pl.pallas_call(
        _post_kernel,
        grid=(t // tm,),
        in_specs=[pl.BlockSpec((tm, d), row), pl.BlockSpec((tm, VA_W), row), pl.BlockSpec((tm, VB_W), row),
                  pl.BlockSpec((tm, GZ_W), row),
                  pl.BlockSpec((1, 8, d), lambda i: (group_of_tile(i), 0, 0)),
                  pl.BlockSpec((2, DV_A), const), pl.BlockSpec((d, d), const), pl.BlockSpec((1, d), const),
                  pl.BlockSpec((d, LANES), const), pl.BlockSpec((1, LANES), const),
                  pl.BlockSpec((tm, tm), const)],
        out_specs=[pl.BlockSpec((tm, d), row), pl.BlockSpec((tm, d), row), pl.BlockSpec((tm, LANES), row),
                   pl.BlockSpec((1, LANES), const)],
        out_shape=[jax.ShapeDtypeStruct((t, d), F32), jax.ShapeDtypeStruct((t, d), F32),
                   jax.ShapeDtypeStruct((t, LANES), F32), jax.ShapeDtypeStruct((1, LANES), F32)],
        compiler_params=_params("arbitrary"),
        name="post_mixer",
    )(x, oa, ob, gz, modtab, gon, w_out, g2.reshape(1, d), rw, rb, lstrict)


def _row_copy(src, src_row, dst, dst_row, sem):
    return pltpu.make_async_copy(src.at[pl.ds(src_row, 1)], dst.at[pl.ds(dst_row, 1)], sem)


def _dispatch_kernel(dest_ref, h2_hbm, xs_in_hbm, xs_hbm, idx_smem, idx_sem, sem):
    del xs_in_hbm
    n = idx_smem.shape[0]
    base = pl.program_id(0) * (n // TOP_K)
    load = pltpu.make_async_copy(dest_ref.at[0, 0], idx_smem, idx_sem)
    load.start()
    load.wait()

    def issue(j, carry):
        _row_copy(h2_hbm, base + lax.shift_right_logical(j, 2), xs_hbm, idx_smem[j], sem).start()
        return carry

    lax.fori_loop(0, n, issue, 0)

    def drain(j, carry):
        _row_copy(h2_hbm, 0, xs_hbm, 0, sem).wait()
        return carry

    lax.fori_loop(0, n, drain, 0)


def _dispatch(dest3, h2, cap):
    t, d = h2.shape
    n = dest3.shape[-1]
    zeros = jnp.zeros((cap, d), F32)
    return pl.pallas_call(
        _dispatch_kernel,
        grid=(dest3.shape[0],),
        in_specs=[pl.BlockSpec((1, 1, n), lambda i: (i, 0, 0)),
                  pl.BlockSpec(memory_space=pl.ANY), pl.BlockSpec(memory_space=pl.ANY)],
        out_specs=pl.BlockSpec(memory_space=pl.ANY),
        out_shape=jax.ShapeDtypeStruct((cap, d), F32),
        scratch_shapes=[pltpu.SMEM((n,), jnp.int32), pltpu.SemaphoreType.DMA, pltpu.SemaphoreType.DMA],
        input_output_aliases={2: 0},
        compiler_params=_params("arbitrary"),
        name="moe_dispatch",
    )(dest3, h2, zeros)


def _expert_kernel(be_ref, nb_ref, xs_ref, wgu_ref, bgu_ref, wd_ref, bd_ref, p_ref, y_ref, wgu_s, wd_s):
    i = pl.program_id(0)
    e = be_ref[i]
    prev = be_ref[jnp.maximum(i - 1, 0)]

    @pl.when(jnp.logical_or(i == 0, e != prev))
    def _():
        p = p_ref[...]
        for j in range(2 * D_FF // (2 * LANES)):
            w = wgu_ref[0, :, j * 2 * LANES:(j + 1) * 2 * LANES].astype(BF16)
            r = jnp.dot(w, p, preferred_element_type=F32).astype(BF16)
            wgu_s[:, j * LANES:(j + 1) * LANES] = r[:, :LANES]
            wgu_s[:, D_FF + j * LANES:D_FF + (j + 1) * LANES] = r[:, LANES:]
        wd_s[...] = wd_ref[0].astype(BF16)

    @pl.when(i < nb_ref[0])
    def _():
        gu = jnp.dot(xs_ref[...].astype(BF16), wgu_s[...], preferred_element_type=F32) + bgu_ref[0]
        gate = jnp.minimum(gu[:, :D_FF], SWIGLU_LIMIT)
        up = jnp.clip(gu[:, D_FF:], -SWIGLU_LIMIT, SWIGLU_LIMIT)
        hdn = (up + 1.0) * (gate * _sigmoid(SWIGLU_ALPHA * gate))
        y_ref[...] = jnp.dot(hdn.astype(BF16), wd_s[...], preferred_element_type=F32) + bd_ref[0]

    @pl.when(i >= nb_ref[0])
    def _():
        y_ref[...] = jnp.zeros_like(y_ref)


def _experts(blk_e, nblk, xs, w_gu, b_gu_r, w_down, b_down):
    cap, d = xs.shape
    bm = EXPERT_BLOCK
    n_e, _, n_gu = w_gu.shape
    blk = lambda i, be, nb: (jnp.minimum(i, nb[0] - 1), 0)
    exp3 = lambda i, be, nb: (be[i], 0, 0)
    grid_spec = pltpu.PrefetchScalarGridSpec(
        num_scalar_prefetch=2,
        grid=(cap // bm,),
        in_specs=[pl.BlockSpec((bm, d), blk),
                  pl.BlockSpec((1, d, n_gu), exp3), pl.BlockSpec((1, 1, n_gu), exp3),
                  pl.BlockSpec((1, D_FF, d), exp3), pl.BlockSpec((1, 1, d), exp3),
                  pl.BlockSpec((2 * LANES, 2 * LANES), lambda i, be, nb: (0, 0))],
        out_specs=pl.BlockSpec((bm, d), lambda i, be, nb: (i, 0)),
        scratch_shapes=[pltpu.VMEM((d, n_gu), BF16), pltpu.VMEM((D_FF, d), BF16)],
    )
    return pl.pallas_call(
        _expert_kernel,
        grid_spec=grid_spec,
        out_shape=jax.ShapeDtypeStruct((cap, d), F32),
        compiler_params=_params("arbitrary"),
        name="moe_experts",
    )(blk_e, nblk, xs, w_gu, b_gu_r.reshape(n_e, 1, n_gu), w_down, b_down.reshape(n_e, 1, d), _deinterleave_matrix())


def _combine_kernel(dest_ref, x1_ref, route_ref, mod_ref, gf_ref, yb_hbm, o_ref, idx_smem, rows, idx_sem, sem,
                    *, final):
    n = idx_smem.shape[0]
    load = pltpu.make_async_copy(dest_ref.at[0, 0], idx_smem, idx_sem)
    load.start()
    load.wait()

    def issue(j, carry):
        _row_copy(yb_hbm, idx_smem[j], rows.at[j & (TOP_K - 1)], lax.shift_right_logical(j, 2), sem).start()
        return carry

    lax.fori_loop(0, n, issue, 0)

    def drain(j, carry):
        _row_copy(yb_hbm, 0, rows.at[0], 0, sem).wait()
        return carry

    lax.fori_loop(0, n, drain, 0)

    route = route_ref[...]
    f = rows[0] * route[:, TOP_K:TOP_K + 1]
    for kk in range(1, TOP_K):
        f = f + rows[kk] * route[:, TOP_K + kk:TOP_K + kk + 1]
    x2 = x1_ref[...] + mod_ref[0, 5:6, :] * f
    o_ref[...] = _rms(x2) * gf_ref[...] if final else x2


def _combine(dest3, x1, route, modtab, g_final, yb, group_of_tile, final):
    t, d = x1.shape
    tm = TOKEN_TILE
    n = dest3.shape[-1]
    row = lambda i: (i, 0)
    return pl.pallas_call(
        functools.partial(_combine_kernel, final=final),
        grid=(t // tm,),
        in_specs=[pl.BlockSpec((1, 1, n), lambda i: (i, 0, 0)),
                  pl.BlockSpec((tm, d), row), pl.BlockSpec((tm, LANES), row),
                  pl.BlockSpec((1, 8, d), lambda i: (group_of_tile(i), 0, 0)),
                  pl.BlockSpec((1, d), lambda i: (0, 0)),
                  pl.BlockSpec(memory_space=pl.ANY)],
        out_specs=pl.BlockSpec((tm, d), row),
        out_shape=jax.ShapeDtypeStruct((t, d), F32),
        scratch_shapes=[pltpu.SMEM((n,), jnp.int32), pltpu.VMEM((TOP_K, tm, d), F32),
                        pltpu.SemaphoreType.DMA, pltpu.SemaphoreType.DMA],
        compiler_params=_params("arbitrary"),
        name="moe_combine",
    )(dest3, x1, route, modtab, g_final.reshape(1, d), yb)


def _grid_swap(x, r1, r2):
    n, d = x.shape
    b = n // (r1 * r2)
    return x.reshape(b, r1, r2, d).swapaxes(1, 2).reshape(n, d)


def _prep_w_in(w_in):
    sizes = (QA_W, QA_W, VA_W, GATE_RANK, VA_W, QKVB_W, 2 * H_B, 2 * H_B, VB_W)
    offs = np.concatenate([[0], np.cumsum(sizes)])
    qa, ka, va, glr, oga, qkvb, ab, bb, zb = [w_in[..., offs[i]:offs[i + 1]] for i in range(len(sizes))]
    pad = jnp.zeros(w_in.shape[:-1] + (SM_W - GATE_RANK - 4 * H_B,), w_in.dtype)
    return jnp.concatenate([qa, ka, va, oga, zb, qkvb, glr, ab, bb, pad], axis=-1).astype(BF16)


def kernel(x_prompt, x_sample, state_gla, state_gdn, c, c_ctx, w_mod, b_mod, g_norm1, g_norm2, w_in,
           gla_w_gate2, gla_b_gate, gla_g_onorm, gdn_conv_w, gdn_a_log, gdn_dt_bias, gdn_g_onorm, w_out,
           router_w, router_b, w_gu, b_gu, w_down, b_down, g_final):
    bp, sp, d = x_prompt.shape
    bs, ns, _ = x_sample.shape
    depth = w_mod.shape[0]
    n_p, n_s = bp * sp, bs * ns
    t = n_p + n_s
    tm = TOKEN_TILE
    assert d == D_MODEL and sp % tm == 0 and ns % tm == 0 and n_p % ns == 0 and bs + 1 <= 8
    rows_s = ns // GRID_W

    def group_of_tile(i):
        return jnp.where(i < n_p // tm, 0, 1 + (i - n_p // tm) // (ns // tm))

    cond = jnp.concatenate([c_ctx[None], c, jnp.zeros((8 - 1 - bs, d), F32)], axis=0)
    mods = _modulation(cond, w_mod, b_mod)
    mods = mods.reshape(depth, 8, N_MOD, d)[:, :1 + bs]
    modtab = jnp.concatenate([mods, jnp.zeros((depth, 1 + bs, 8 - N_MOD, d), F32)], axis=2)

    w_in_r = _prep_w_in(w_in)
    w_out_b = w_out.astype(BF16)
    wg = jnp.concatenate([gla_w_gate2, jnp.zeros((depth, 2, SM_W - GATE_RANK, QA_W), F32)], axis=2)
    bg = gla_b_gate.reshape(depth, 2, 1, QA_W)
    lane_pad = lambda v, off: jnp.pad(v.reshape(depth, 1, -1), ((0, 0), (0, 0), (off, SM_W - off - v[0].size)))
    a_log_vec = lane_pad(gdn_a_log, SM_A0)
    dtb_vec = lane_pad(gdn_dt_bias, SM_A0)
    conv_w = jnp.pad(gdn_conv_w, ((0, 0), (0, 8 - gdn_conv_w.shape[1]), (0, 0)))
    gon = jnp.stack([gla_g_onorm, gdn_g_onorm], axis=1)
    rw = jnp.pad(router_w, ((0, 0), (0, 0), (0, LANES - N_EXPERTS)))
    rb = jnp.pad(router_b, ((0, 0), (0, LANES - N_EXPERTS)), constant_values=-1e30).reshape(depth, 1, LANES)
    b_gu_r = jnp.concatenate([b_gu[..., 0::2], b_gu[..., 1::2]], axis=-1)

    hsel = jnp.eye(H_A, dtype=F32)
    st_a_s = jnp.einsum("bldhkv,hg->bldhvgk", state_gla.astype(F32), hsel).reshape(bs, depth, 2, VA_W, QA_W)
    st_b_s = state_gdn.astype(F32)
    st_a_0 = jnp.zeros((1, 2, VA_W, QA_W), F32)
    st_b_0 = jnp.zeros((1, 2, H_B, DK_B, DV_B), F32)

    n_blocks = (t * TOP_K) // EXPERT_BLOCK + N_EXPERTS
    cap = n_blocks * EXPERT_BLOCK

    x = jnp.concatenate([x_prompt.reshape(n_p, d), x_sample.reshape(n_s, d)], axis=0)
    new_gla, new_gdn = [], []
    for l in range(depth):
        if l % 2 == 1:
            x = jnp.concatenate([x[:n_p], _grid_swap(x[n_p:], rows_s, GRID_W)], axis=0)
        pa, gz, pb, sm = _pre_mixer(x, modtab[l], g_norm1[l], w_in_r[l], group_of_tile)
        oa_p, sa = _gla(pa, sm, st_a_0, wg[l], bg[l], nseq=bp, seqlen=sp, row_block0=0, shared_state=True)
        oa_s, _ = _gla(pa, sm, st_a_s[:, l], wg[l], bg[l], nseq=bs, seqlen=ns, row_block0=n_p // ns,
                       shared_state=False)
        ob_p, sb = _gdn(pb, sm, st_b_0, conv_w[l], a_log_vec[l], dtb_vec[l], nseq=bp, seqlen=sp, row_block0=0,
                        shared_state=True)
        ob_s, _ = _gdn(pb, sm, st_b_s[:, l], conv_w[l], a_log_vec[l], dtb_vec[l], nseq=bs, seqlen=ns,
                       row_block0=n_p // ns, shared_state=False)
        new_gla.append(sa)
        new_gdn.append(sb)
        oa = jnp.concatenate([oa_p, oa_s], axis=0)
        ob = jnp.concatenate([ob_p, ob_s], axis=0)
        x1, h2, route, counts = _post_mixer(x, oa, ob, gz, modtab[l], gon[l], w_out_b[l], g_norm2[l],
                                            rw[l], rb[l], group_of_tile)

        top_i = route[:, :TOP_K].astype(jnp.int32)
        rank = route[:, 2 * TOP_K:3 * TOP_K].astype(jnp.int32)
        sizes = counts[0, :N_EXPERTS].astype(jnp.int32)
        padded = (sizes + EXPERT_BLOCK - 1) // EXPERT_BLOCK * EXPERT_BLOCK
        pend = jnp.cumsum(padded)
        pstart = pend - padded
        onehot = top_i[..., None] == jnp.arange(N_EXPERTS, dtype=jnp.int32)
        dest = rank + jnp.sum(jnp.where(onehot, pstart, 0), axis=-1)
        dest3 = dest.reshape(t // tm, 1, tm * TOP_K)
        nblk = (pend[-1] // EXPERT_BLOCK).reshape(1)
        blk_start = jnp.minimum(jnp.arange(n_blocks, dtype=jnp.int32), nblk[0] - 1) * EXPERT_BLOCK
        blk_e = jnp.sum(blk_start[:, None] >= pend[None, :], axis=-1).astype(jnp.int32)

        xs = _dispatch(dest3, h2, cap)
        yb = _experts(blk_e, nblk, xs, w_gu[l], b_gu_r[l], w_down[l], b_down[l])
        x = _combine(dest3, x1, route, modtab[l], g_final, yb, group_of_tile, final=(l == depth - 1))
        if l % 2 == 1:
            x = jnp.concatenate([x[:n_p], _grid_swap(x[n_p:], GRID_W, rows_s)], axis=0)

    y_prompt = x[:n_p].reshape(bp, sp, d)
    y_sample = x[n_p:].reshape(bs, ns, d)
    sa_all = jnp.stack(new_gla, axis=1).reshape(bp, depth, 2, H_A, DV_A, H_A, DK_A)
    new_state_gla = jnp.einsum("bldhvgk,hg->bldhkv", sa_all, hsel)
    new_state_gdn = jnp.stack(new_gdn, axis=1)
    return (y_prompt, y_sample, new_state_gla, new_state_gdn)
```

```python
import functools

import numpy as np
import jax
import jax.numpy as jnp
from jax import lax
from jax.experimental import pallas as pl
from jax.experimental.pallas import tpu as pltpu

F32 = jnp.float32
BF16 = jnp.bfloat16

D_MODEL = 1024
GRID_W = 64
H_A, DK_A, DV_A = 4, 64, 128
H_B, DK_B, DV_B = 4, 128, 128
GATE_RANK = 16
GATE_NORMALIZER = 16.0
CHUNK = 64
N_EXPERTS = 32
TOP_K = 4
D_FF = 1024
SWIGLU_LIMIT = 7.0
SWIGLU_ALPHA = 1.702
N_MOD = 6
EPS = 1e-6

QA_W = H_A * DK_A
VA_W = H_A * DV_A
QB_W = H_B * DK_B
VB_W = H_B * DV_B
QKVB_W = 2 * QB_W + VB_W
PA_W = 2 * QA_W + VA_W
GZ_W = VA_W + VB_W
SM_W = 128
SM_A0 = GATE_RANK
SM_B0 = GATE_RANK + 2 * H_B
W_IN_COLS = PA_W + GZ_W + QKVB_W + SM_W

TOKEN_TILE = 256
EXPERT_BLOCK = 256
LANES = 128
SUBLANES = 8
ROW_TILES = D_MODEL // LANES
BF16_ROWS = 16
VMEM_LIMIT_BYTES = 56 * 2**20
SEQS_PER_STEP_SHORT = 4
SEQS_PER_STEP_LONG = 2

_NT = (((1,), (1,)), ((), ()))
_TN = (((0,), (0,)), ((), ()))

assert ROW_TILES == SUBLANES


def _params(*sem):
    return pltpu.CompilerParams(dimension_semantics=sem, vmem_limit_bytes=VMEM_LIMIT_BYTES)


def _dot(a, b):
    return jnp.dot(a.astype(BF16), b.astype(BF16), preferred_element_type=F32)


def _dot_nt(a, b):
    return lax.dot_general(a.astype(BF16), b.astype(BF16), _NT, preferred_element_type=F32)


def _dot_tn(a, b):
    return lax.dot_general(a.astype(BF16), b.astype(BF16), _TN, preferred_element_type=F32)


def _split(x):
    hi = x.astype(BF16)
    lo = (x - hi.astype(F32)).astype(BF16)
    return hi, lo


def _dot_x2(m, x):
    hi, lo = _split(x)
    return _dot(m, hi) + _dot(m, lo)


def _dot_2x(x, m):
    hi, lo = _split(x)
    return _dot(hi, m) + _dot(lo, m)


def _dot3(a, b):
    ah, al = _split(a)
    bh, bl = _split(b)
    return _dot(ah, bh) + _dot(ah, bl) + _dot(al, bh)


def _sigmoid(x):
    return 1.0 / (1.0 + jnp.exp(-x))


def _softplus(x):
    return jnp.maximum(x, 0.0) + jnp.log(1.0 + jnp.exp(-jnp.abs(x)))


def _rms(x):
    return x * lax.rsqrt(jnp.mean(x * x, axis=-1, keepdims=True) + EPS)


def _stack4(x, mask):
    xb = x.astype(BF16)
    return jnp.concatenate([xb, xb, xb, xb], axis=0) * mask


def _to_rows(tiles_ref, n):
    return jnp.concatenate([tiles_ref[pl.ds(j, n, stride=ROW_TILES), :] for j in range(ROW_TILES)], axis=1)


def _store_rows(tiles_ref, x):
    n = x.shape[0]
    for j in range(ROW_TILES):
        tiles_ref[pl.ds(j, n, stride=ROW_TILES), :] = x[:, j * LANES:(j + 1) * LANES]


def _block_mask(rows, row_blk, cols, col_blk):
    r = np.arange(rows)[:, None] // row_blk
    c = np.arange(cols)[None, :] // col_blk
    return (r == c).astype(np.float32)


@functools.lru_cache(maxsize=None)
def _gla_consts():
    c = CHUNK
    w = np.zeros((7, c, c), np.float32)
    m = np.zeros((7, c, c), np.float32)
    w[0] = np.tril(np.ones((c, c)))
    m[0] = np.eye(c)
    i = np.arange(c)
    for l, s in enumerate((32, 16, 8, 4, 2, 1), start=1):
        ref = (i // (2 * s)) * 2 * s + s - 1
        for r in range(c):
            if r > ref[r]:
                w[l, r, ref[r] + 1:r + 1] = 1.0
            else:
                w[l, r, r + 1:ref[r] + 1] = 1.0
        same = (i[:, None] // (2 * s)) == (i[None, :] // (2 * s))
        m[l] = same & ((i[:, None] % (2 * s)) >= s) & ((i[None, :] % (2 * s)) < s)
    assert np.array_equal(m.sum(0), np.tril(np.ones((c, c))))
    wall = np.stack([w.reshape(7 * c, c), w[:, ::-1, ::-1].reshape(7 * c, c)])
    masks = np.stack([np.tile(m, (1, 1, H_A)), np.tile(m[:, ::-1, ::-1], (1, 1, H_A))])
    return dict(
        wall=jnp.asarray(wall, BF16),
        masks=jnp.asarray(masks, F32),
        hm=jnp.asarray(_block_mask(H_A * c, c, QA_W, DK_A), BF16),
        vm=jnp.asarray(_block_mask(H_A * c, c, VA_W, DV_A), BF16),
        bd=jnp.asarray(_block_mask(VA_W, DV_A, QA_W, DK_A), F32),
    )


@functools.lru_cache(maxsize=None)
def _gdn_consts():
    c = CHUNK
    tril = np.tril(np.ones((c, c), np.float32))
    lc = np.stack([tril, tril.T])
    ut = np.stack([np.tile(lc[d].T, (1, H_B)) for d in range(2)])
    cm = np.stack([np.tile(lc[d], (1, H_B)) for d in range(2)])
    eye = np.tile(np.eye(c, dtype=np.float32), (1, H_B))
    sm = cm - eye[None]
    i = np.arange(c)
    dblk = np.tile(((i[:, None] // 16) == (i[None, :] // 16)).astype(np.float32), (1, H_B))
    eg = np.zeros((2, SM_W, H_B * c), np.float32)
    eb = np.zeros((2, SM_W, H_B * c), np.float32)
    egw = np.zeros((2, SM_W, QB_W), np.float32)
    ebw = np.zeros((2, SM_W, QB_W), np.float32)
    for d in range(2):
        for h in range(H_B):
            eg[d, SM_A0 + d * H_B + h, h * c:(h + 1) * c] = 1.0
            eb[d, SM_B0 + d * H_B + h, h * c:(h + 1) * c] = 1.0
            egw[d, SM_A0 + d * H_B + h, h * DK_B:(h + 1) * DK_B] = 1.0
            ebw[d, SM_B0 + d * H_B + h, h * DK_B:(h + 1) * DK_B] = 1.0
    return dict(
        lc=jnp.asarray(lc, BF16), ut=jnp.asarray(ut, F32), cm=jnp.asarray(cm, F32), sm=jnp.asarray(sm, F32),
        eye=jnp.asarray(eye, F32), dblk=jnp.asarray(dblk, F32),
        eg=jnp.asarray(eg, BF16), eb=jnp.asarray(eb, BF16), egw=jnp.asarray(egw, BF16), ebw=jnp.asarray(ebw, BF16),
        hm=jnp.asarray(_block_mask(H_B * c, c, H_B * c, c), BF16),
        km=jnp.asarray(_block_mask(H_B * c, c, QB_W, DK_B), BF16),
    )


@functools.lru_cache(maxsize=None)
def _deinterleave_matrix():
    p = np.zeros((2 * LANES, 2 * LANES), np.float32)
    j = np.arange(LANES)
    p[2 * j, j] = 1.0
    p[2 * j + 1, LANES + j] = 1.0
    return jnp.asarray(p, BF16)


def _mod_kernel(c_ref, w_ref, b_ref, o_ref):
    c = c_ref[...]
    o_ref[0] = _dot3(c * _sigmoid(c), w_ref[0]) + b_ref[0]


def _modulation(cond, w_mod, b_mod):
    depth, d, n = w_mod.shape
    tn = 1536
    return pl.pallas_call(
        _mod_kernel,
        grid=(depth, n // tn),
        in_specs=[pl.BlockSpec((8, d), lambda l, j: (0, 0)),
                  pl.BlockSpec((1, d, tn), lambda l, j: (l, 0, j)),
                  pl.BlockSpec((1, 1, tn), lambda l, j: (l, 0, j))],
        out_specs=pl.BlockSpec((1, 8, tn), lambda l, j: (l, 0, j)),
        out_shape=jax.ShapeDtypeStruct((depth, 8, n), F32),
        compiler_params=_params("parallel", "parallel"),
        name="modulation",
    )(cond, w_mod, b_mod.reshape(depth, 1, n))


def _pre_kernel(x_ref, mod_ref, g_ref, w_ref, pa_ref, gz_ref, pb_ref, sm_ref):
    h = _rms(x_ref[...]) * g_ref[...]
    h = h * (1.0 + mod_ref[0, 1:2, :]) + mod_ref[0, 0:1, :]
    p = jnp.dot(h.astype(BF16), w_ref[...], preferred_element_type=F32)
    pa_ref[...] = p[:, :PA_W].astype(BF16)
    gz_ref[...] = p[:, PA_W:PA_W + GZ_W].astype(BF16)
    pb_ref[...] = p[:, PA_W + GZ_W:PA_W + GZ_W + QKVB_W].astype(BF16)
    sm_ref[...] = p[:, PA_W + GZ_W + QKVB_W:]


def _pre_mixer(x, modtab, g1, w_in_r, group_of_tile):
    t, d = x.shape
    tm = TOKEN_TILE
    row = lambda i: (i, 0)
    return pl.pallas_call(
        _pre_kernel,
        grid=(t // tm,),
        in_specs=[pl.BlockSpec((tm, d), row),
                  pl.BlockSpec((1, 8, d), lambda i: (group_of_tile(i), 0, 0)),
                  pl.BlockSpec((1, d), lambda i: (0, 0)),
                  pl.BlockSpec((d, W_IN_COLS), lambda i: (0, 0))],
        out_specs=[pl.BlockSpec((tm, PA_W), row), pl.BlockSpec((tm, GZ_W), row),
                   pl.BlockSpec((tm, QKVB_W), row), pl.BlockSpec((tm, SM_W), row)],
        out_shape=[jax.ShapeDtypeStruct((t, PA_W), BF16), jax.ShapeDtypeStruct((t, GZ_W), BF16),
                   jax.ShapeDtypeStruct((t, QKVB_W), BF16), jax.ShapeDtypeStruct((t, SM_W), F32)],
        compiler_params=_params("parallel"),
        name="pre_mixer",
    )(x, modtab, g1.reshape(1, d), w_in_r)


def _gla_kernel(pa_ref, sm_ref, wall_ref, mask_ref, wgh_ref, wgl_ref, bg_ref, hm_ref, vm_ref, bd_ref, st0_ref,
                o_ref, stf_ref, st_scr, *, nchunk, nseq):
    c = CHUNK
    seqlen = nchunk * c
    hm = hm_ref[...]
    vm = vm_ref[...]
    bd = bd_ref[...]
    st_scr[...] = st0_ref[...]

    chains = [(s, d) for s in range(nseq) for d in range(2)]
    dirs = [d for _, d in chains]

    def body(ci, carry):
        rows = [pl.ds(pl.multiple_of(s * seqlen + (ci if d == 0 else nchunk - 1 - ci) * c, c), c)
                for s, d in chains]
        q = [pa_ref[r, 0:QA_W].astype(F32) * (DK_A ** -0.5) for r in rows]
        k = [pa_ref[r, QA_W:2 * QA_W].astype(F32) for r in rows]
        v = [pa_ref[r, 2 * QA_W:PA_W] for r in rows]
        sm = [_split(sm_ref[r, :]) for r in rows]
        x = _each(lambda y, d: _dot(y[0], wgh_ref[d]) + _dot(y[0], wgl_ref[d]) + _dot(y[1], wgh_ref[d]) + bg_ref[d],
                  sm, dirs)
        gk = [(jnp.minimum(y, 0.0) - jnp.log(1.0 + jnp.exp(-jnp.abs(y)))) * (1.0 / GATE_NORMALIZER) for y in x]
        e_all = _each(lambda g, d: _dot_x2(wall_ref[d], g), gk, dirs)
        a = None
        for l in range(7):
            if l == 0:
                qt, kt = q, k
            else:
                f = [jnp.exp(e[l * c:(l + 1) * c]) for e in e_all]
                qt = _each(lambda y, z: y * z, q, f)
                kt = _each(lambda y, z: y * z, k, f)
            r = _each(lambda y, z, d: _dot_nt(y, _stack4(z, hm)) * mask_ref[d, l], qt, kt, dirs)
            a = r if a is None else _each(lambda y, z: y + z, a, r)
        b = [e[0:c] for e in e_all]
        o = _each(lambda aa, vv, qq, bb, sd: _dot(aa, _stack4(vv, vm)) + _dot_nt(qq * jnp.exp(bb), st_scr[sd]),
                  a, v, q, b, chains)
        for r, d, y in zip(rows, dirs, o):
            o_ref[r, d * VA_W:(d + 1) * VA_W] = y
        b_last = _each(lambda bb, d: bb[c - 1:c] if d == 0 else bb[0:1], b, dirs)
        upd = _each(lambda vv, kk, bl, bb: _dot_tn(vv, kk * jnp.exp(bl - bb)), v, k, b_last, b)
        for sd, bl, up in zip(chains, b_last, upd):
            st_scr[sd] = st_scr[sd] * jnp.exp(bl) + up * bd
        return carry

    lax.fori_loop(0, nchunk, body, 0)
    stf_ref[...] = st_scr[...]


def _gla(pa, sm, st0, wgh, wgl, bg, *, ntotal, seqlen, nseq, row_block0, shared_state):
    cst = _gla_consts()
    nchunk = seqlen // CHUNK
    rows = nseq * seqlen
    seq = lambda b: (row_block0 + b, 0)
    full = lambda *shape: pl.BlockSpec(shape, lambda b: (0,) * len(shape))
    st_in = (lambda b: (0, 0, 0, 0)) if shared_state else (lambda b: (b, 0, 0, 0))
    return pl.pallas_call(
        functools.partial(_gla_kernel, nchunk=nchunk, nseq=nseq),
        grid=(ntotal // nseq,),
        in_specs=[pl.BlockSpec((rows, PA_W), seq), pl.BlockSpec((rows, SM_W), seq),
                  full(2, 7 * CHUNK, CHUNK), full(2, 7, CHUNK, QA_W),
                  full(2, SM_W, QA_W), full(2, SM_W, QA_W), full(2, 1, QA_W),
                  full(QA_W, QA_W), full(QA_W, VA_W), full(VA_W, QA_W),
                  pl.BlockSpec((nseq, 2, VA_W, QA_W), st_in)],
        out_specs=[pl.BlockSpec((rows, 2 * VA_W), lambda b: (b, 0)),
                   pl.BlockSpec((nseq, 2, VA_W, QA_W), lambda b: (b, 0, 0, 0))],
        out_shape=[jax.ShapeDtypeStruct((ntotal * seqlen, 2 * VA_W), F32),
                   jax.ShapeDtypeStruct((ntotal, 2, VA_W, QA_W), F32)],
        scratch_shapes=[pltpu.VMEM((nseq, 2, VA_W, QA_W), F32)],
        compiler_params=_params("parallel"),
        name="gla_l%d" % seqlen,
    )(pa, sm, cst["wall"], cst["masks"], wgh, wgl, bg, cst["hm"], cst["vm"], cst["bd"], st0)


def _each(f, *lists):
    return [f(*xs) for xs in zip(*lists)]


def _mm_heads(xs, ys, hm):
    xs = _each(_split, xs)
    ys = _each(_split, ys)
    yh4 = [_stack4(y[0], hm) for y in ys]
    yl4 = [_stack4(y[1], hm) for y in ys]
    p0 = _each(lambda x, y: _dot(x[0], y), xs, yh4)
    p1 = _each(lambda x, y: _dot(x[0], y), xs, yl4)
    p2 = _each(lambda x, y: _dot(x[1], y), xs, yh4)
    return _each(lambda a, b, c: a + b + c, p0, p1, p2)


def _unit_tri_inverse(ms, eye, dblk, hm):
    mm = functools.partial(_mm_heads, hm=hm)
    dg = [m * dblk for m in ms]
    lo = _each(lambda m, g: m - g, ms, dg)
    d2 = mm(dg, dg)
    d3 = mm(dg, d2)
    d4 = mm(d2, d2)
    p1 = _each(lambda g, a, b: eye - g + a - b, dg, d2, d3)
    p2 = _each(lambda a, b: a + b, p1, mm(p1, d4))
    d8 = mm(d4, d4)
    td = _each(lambda a, b: a + b, p2, mm(p2, d8))
    e1 = mm(td, lo)
    e2 = mm(e1, e1)
    e3 = mm(e1, e2)
    return mm(_each(lambda a, b, c: eye - a + b - c, e1, e2, e3), td)


def _gdn_kernel(pb_ref, sm_ref, cw_ref, al_ref, dtb_ref, lc_ref, ut_ref, cm_ref, smk_ref, eye_ref, dblk_ref,
                eg_ref, eb_ref, egw_ref, ebw_ref, hm_ref, km_ref, st0_ref, o_ref, stf_ref, qkv_scr, s_scr,
                *, nchunk, nseq):
    c = CHUNK
    seqlen = nchunk * c
    w0 = cw_ref[0:1, :]
    w1 = cw_ref[1:2, :]
    w2 = cw_ref[2:3, :]

    def conv_body(cc, carry):
        for s in range(nseq):
            r0 = pl.multiple_of(s * seqlen + cc * c, c)
            x = pb_ref[pl.ds(r0, c), :].astype(F32)
            p0 = pl.multiple_of(jnp.maximum(r0 - BF16_ROWS, 0), BF16_ROWS)
            n0 = pl.multiple_of(jnp.minimum(r0 + c, nseq * seqlen - BF16_ROWS), BF16_ROWS)
            pg = pb_ref[pl.ds(p0, BF16_ROWS), :]
            ng = pb_ref[pl.ds(n0, BF16_ROWS), :]
            prow = jnp.where(cc > 0, pg.astype(F32)[BF16_ROWS - 1:BF16_ROWS], 0.0)
            nrow = jnp.where(cc < nchunk - 1, ng.astype(F32)[0:1], 0.0)
            rid = lax.broadcasted_iota(jnp.int32, x.shape, 0)
            xp = jnp.where(rid == 0, prow, pltpu.roll(x, 1, axis=0))
            xn = jnp.where(rid == c - 1, nrow, pltpu.roll(x, c - 1, axis=0))
            y = w0 * xp + w1 * x + w2 * xn
            sl = y * _sigmoid(y)
            parts = []
            for h in range(2 * H_B):
                p = sl[:, h * DK_B:(h + 1) * DK_B]
                nrm = lax.rsqrt(jnp.sum(p * p, axis=-1, keepdims=True) + EPS)
                parts.append(p * (nrm * (DK_B ** -0.5 if h < H_B else 1.0)))
            parts.append(sl[:, 2 * QB_W:])
            qkv_scr[pl.ds(r0, c), :] = jnp.concatenate(parts, axis=1).astype(BF16)
        return carry

    lax.fori_loop(0, nchunk, conv_body, 0)

    hm = hm_ref[...]
    km = km_ref[...]
    eye = eye_ref[...]
    dblk = dblk_ref[...]
    ones = jnp.ones((c, c), BF16)
    neg_a = -jnp.exp(al_ref[...])
    dtb = dtb_ref[...]
    s_scr[...] = st0_ref[...]

    chains = [(s, d) for s in range(nseq) for d in range(2)]
    dirs = [d for _, d in chains]
    heads = [slice(h * DK_B, (h + 1) * DK_B) for h in range(H_B)]

    def body(ci, carry):
        rows = [pl.ds(pl.multiple_of(s * seqlen + (ci if d == 0 else nchunk - 1 - ci) * c, c), c)
                for s, d in chains]
        q = [qkv_scr[r, 0:QB_W].astype(F32) for r in rows]
        k = [qkv_scr[r, QB_W:2 * QB_W].astype(F32) for r in rows]
        v = [qkv_scr[r, 2 * QB_W:QKVB_W].astype(F32) for r in rows]
        sm = [sm_ref[r, :] for r in rows]
        g_all = [_split(neg_a * _softplus(x + dtb)) for x in sm]
        beta_all = [_split(_sigmoid(x)) for x in sm]
        two = lambda ref: (lambda x, d: _dot(x[0], ref[d]) + _dot(x[1], ref[d]))
        g_b = _each(two(eg_ref), g_all, dirs)
        g_w = _each(two(egw_ref), g_all, dirs)
        beta_i = _each(two(eb_ref), beta_all, dirs)
        beta_w = _each(two(ebw_ref), beta_all, dirs)
        gam_i = _each(lambda x, d: _dot_x2(lc_ref[d], x), g_b, dirs)
        gam_j = _each(lambda x, d: _dot_x2(ones, x * ut_ref[d]), g_b, dirs)
        gam_w = _each(lambda x, d: _dot_x2(lc_ref[d], x), g_w, dirs)
        dec = _each(lambda a, b, d: jnp.exp(jnp.minimum(a - b, 0.0)) * cm_ref[d], gam_i, gam_j, dirs)
        kkqk = _each(lambda kk, qq: _dot_nt(jnp.concatenate([kk, qq], axis=0), _stack4(kk, km)), k, q)
        m = _each(lambda bi, x, dc, d: smk_ref[d] * bi * x[:c] * dc, beta_i, kkqk, dec, dirs)
        t_inv = _unit_tri_inverse(m, eye, dblk, hm)
        eg_w = [jnp.exp(x) for x in gam_w]
        kh = _each(lambda kk, bw, e: kk * (bw * e), k, beta_w, eg_w)
        qh = _each(lambda qq, e: qq * e, q, eg_w)
        res = [[_dot(jnp.concatenate([a[:, hs], b[:, hs]], axis=0), s_scr[s, d, h])
                for h, hs in enumerate(heads)] for a, b, (s, d) in zip(kh, qh, chains)]
        rhs = _each(lambda vv, bw, r: vv * bw - jnp.concatenate([x[:c] for x in r], axis=1), v, beta_w, res)
        u = _each(lambda t, r: _dot(t, _stack4(r, km)), t_inv, rhs)
        o = _each(lambda r, x, dc, uu: jnp.concatenate([y[c:] for y in r], axis=1)
                  + _dot(x[c:] * dc, _stack4(uu, km)), res, kkqk, dec, u)
        for r, d, x in zip(rows, dirs, o):
            o_ref[r, d * VB_W:(d + 1) * VB_W] = x
        g_last = _each(lambda x, d: x[c - 1:c] if d == 0 else x[0:1], gam_w, dirs)
        kd = _each(lambda kk, gl, gw: kk * jnp.exp(gl - gw), k, g_last, gam_w)
        upd = [[_dot_tn(a[:, hs], b[:, hs]) for hs in heads] for a, b in zip(kd, u)]
        for (s, d), gl, up in zip(chains, g_last, upd):
            for h, hs in enumerate(heads):
                s_scr[s, d, h] = s_scr[s, d, h] * jnp.exp(gl[:, hs]) + up[h]
        return carry

    lax.fori_loop(0, nchunk, body, 0)
    stf_ref[...] = s_scr[...]


def _gdn(pb, sm, st0, conv_w, a_log_vec, dtb_vec, *, ntotal, seqlen, nseq, row_block0, shared_state):
    cst = _gdn_consts()
    nchunk = seqlen // CHUNK
    rows = nseq * seqlen
    hc = H_B * CHUNK
    seq = lambda b: (row_block0 + b, 0)
    full = lambda *shape: pl.BlockSpec(shape, lambda b: (0,) * len(shape))
    st_in = (lambda b: (0, 0, 0, 0, 0)) if shared_state else (lambda b: (b, 0, 0, 0, 0))
    return pl.pallas_call(
        functools.partial(_gdn_kernel, nchunk=nchunk, nseq=nseq),
        grid=(ntotal // nseq,),
        in_specs=[pl.BlockSpec((rows, QKVB_W), seq), pl.BlockSpec((rows, SM_W), seq),
                  full(8, QKVB_W), full(1, SM_W), full(1, SM_W),
                  full(2, CHUNK, CHUNK), full(2, CHUNK, hc), full(2, CHUNK, hc), full(2, CHUNK, hc),
                  full(CHUNK, hc), full(CHUNK, hc),
                  full(2, SM_W, hc), full(2, SM_W, hc), full(2, SM_W, QB_W), full(2, SM_W, QB_W),
                  full(hc, hc), full(hc, QB_W),
                  pl.BlockSpec((nseq, 2, H_B, DK_B, DV_B), st_in)],
        out_specs=[pl.BlockSpec((rows, 2 * VB_W), lambda b: (b, 0)),
                   pl.BlockSpec((nseq, 2, H_B, DK_B, DV_B), lambda b: (b, 0, 0, 0, 0))],
        out_shape=[jax.ShapeDtypeStruct((ntotal * seqlen, 2 * VB_W), F32),
                   jax.ShapeDtypeStruct((ntotal, 2, H_B, DK_B, DV_B), F32)],
        scratch_shapes=[pltpu.VMEM((rows, QKVB_W), BF16), pltpu.VMEM((nseq, 2, H_B, DK_B, DV_B), F32)],
        compiler_params=_params("parallel"),
        name="gdn_l%d" % seqlen,
    )(pb, sm, conv_w, a_log_vec, dtb_vec, cst["lc"], cst["ut"], cst["cm"], cst["sm"], cst["eye"], cst["dblk"],
      cst["eg"], cst["eb"], cst["egw"], cst["ebw"], cst["hm"], cst["km"], st0)


def _post_kernel(x_ref, oa_ref, ob_ref, gz_ref, mod_ref, gon_ref, wo_ref, g2_ref, rw_ref, rb_ref, ls_ref,
                 x1_ref, h2_ref, route_ref, cnt_ref):
    @pl.when(pl.program_id(0) == 0)
    def _():
        cnt_ref[...] = jnp.zeros_like(cnt_ref)

    gz = gz_ref[...]
    heads = []
    for grp, o_ref in enumerate((oa_ref, ob_ref)):
        o = o_ref[:, :VA_W] + o_ref[:, VA_W:]
        for h in range(H_A):
            hs = slice(h * DV_A, (h + 1) * DV_A)
            gate = gz[:, grp * VA_W + h * DV_A:grp * VA_W + (h + 1) * DV_A].astype(F32)
            heads.append(_rms(o[:, hs]) * gon_ref[grp:grp + 1, :] * (gate * _sigmoid(gate)))
    mixed = jnp.concatenate(heads, axis=1)
    y = jnp.dot(mixed.astype(BF16), wo_ref[...], preferred_element_type=F32)
    x1 = x_ref[...] + mod_ref[0, 2:3, :] * y
    x1_ref[...] = x1
    h2 = _rms(x1) * g2_ref[...]
    h2 = h2 * (1.0 + mod_ref[0, 4:5, :]) + mod_ref[0, 3:4, :]
    _store_rows(h2_ref, h2)

    logits = _dot3(h2, rw_ref[...]) + rb_ref[...]
    lane = lax.broadcasted_iota(jnp.int32, logits.shape, 1).astype(F32)
    vals, idxs = [], []
    rem = logits
    for _ in range(TOP_K):
        mx = jnp.max(rem, axis=-1, keepdims=True)
        ix = jnp.min(jnp.where(rem == mx, lane, float(LANES)), axis=-1, keepdims=True)
        vals.append(mx)
        idxs.append(ix)
        rem = jnp.where(lane == ix, -jnp.inf, rem)
    exps = [jnp.exp(v - vals[0]) for v in vals]
    inv = 1.0 / (exps[0] + exps[1] + exps[2] + exps[3])
    onehot = jnp.zeros(logits.shape, F32)
    for ix in idxs:
        onehot = onehot + (lane == ix).astype(F32)
    before = _dot(ls_ref[...], onehot) + cnt_ref[...]
    route = jnp.zeros(logits.shape, F32)
    for kk in range(TOP_K):
        rank = jnp.sum(jnp.where(lane == idxs[kk], before, 0.0), axis=-1, keepdims=True)
        route = jnp.where(lane == kk, idxs[kk], route)
        route = jnp.where(lane == TOP_K + kk, exps[kk] * inv, route)
        route = jnp.where(lane == 2 * TOP_K + kk, rank, route)
    route_ref[...] = route
    cnt_ref[...] = cnt_ref[...] + jnp.sum(onehot, axis=0, keepdims=True)


def _post_mixer(x, oa, ob, gz, modtab, gon, w_out, g2, rw, rb, group_of_tile):
    t, d = x.shape
    tm = TOKEN_TILE
    row = lambda i: (i, 0)
    const = lambda i: (0, 0)
    lstrict = jnp.asarray(np.tril(np.ones((tm, tm), np.float32), -1), BF16)
    return pl.pallas_call(
        _post_kernel,
        grid=(t // tm,),
        in_specs=[pl.BlockSpec((tm, d), row), pl.BlockSpec((tm, 2 * VA_W), row), pl.BlockSpec((tm, 2 * VB_W), row),
                  pl.BlockSpec((tm, GZ_W), row),
                  pl.BlockSpec((1, 8, d), lambda i: (group_of_tile(i), 0, 0)),
                  pl.BlockSpec((2, DV_A), const), pl.BlockSpec((d, d), const), pl.BlockSpec((1, d), const),
                  pl.BlockSpec((d, LANES), const), pl.BlockSpec((1, LANES), const),
                  pl.BlockSpec((tm, tm), const)],
        out_specs=[pl.BlockSpec((tm, d), row), pl.BlockSpec((tm * ROW_TILES, LANES), row),
                   pl.BlockSpec((tm, LANES), row), pl.BlockSpec((1, LANES), const)],
        out_shape=[jax.ShapeDtypeStruct((t, d), F32), jax.ShapeDtypeStruct((t * ROW_TILES, LANES), F32),
                   jax.ShapeDtypeStruct((t, LANES), F32), jax.ShapeDtypeStruct((1, LANES), F32)],
        compiler_params=_params("arbitrary"),
        name="post_mixer",
    )(x, oa, ob, gz, modtab, gon, w_out, g2.reshape(1, d), rw, rb, lstrict)


def _tile_copy(src, src_row, dst, dst_row, sem):
    return pltpu.make_async_copy(src.at[pl.ds(pl.multiple_of(src_row, SUBLANES), SUBLANES)],
                                 dst.at[pl.ds(pl.multiple_of(dst_row, SUBLANES), SUBLANES)], sem)


def _load_indices(idx_vmem_ref, idx_smem, sem):
    load = pltpu.make_async_copy(idx_vmem_ref.at[0, 0], idx_smem, sem)
    load.start()
    load.wait()


def _expert_kernel(be_ref, nb_ref, tok0_ref, tokn_ref, h2_hbm, wgu_ref, bgu_ref, wd_ref, bd_ref, p_ref,
                   y_ref, wgu_s, wd_s, xbuf, idx_smem, idx_sem, sems):
    i = pl.program_id(0)
    nb = nb_ref[0]
    bm = EXPERT_BLOCK
    slot = lax.rem(i, 2)

    def gather(tok_ref, into):
        _load_indices(tok_ref, idx_smem, idx_sem)
        for r in range(bm):
            _tile_copy(h2_hbm, idx_smem[r], xbuf.at[into], r * SUBLANES, sems.at[into]).start()

    @pl.when(i == 0)
    def _():
        gather(tok0_ref, 0)

    @pl.when(i + 1 < nb)
    def _():
        gather(tokn_ref, 1 - slot)

    e = be_ref[i]
    prev = be_ref[jnp.maximum(i - 1, 0)]

    @pl.when(jnp.logical_or(i == 0, e != prev))
    def _():
        p = p_ref[...]
        for j in range(2 * D_FF // (2 * LANES)):
            w = wgu_ref[0, 0, :, j * 2 * LANES:(j + 1) * 2 * LANES].astype(BF16)
            r = jnp.dot(w, p, preferred_element_type=F32).astype(BF16)
            wgu_s[:, j * LANES:(j + 1) * LANES] = r[:, :LANES]
            wgu_s[:, D_FF + j * LANES:D_FF + (j + 1) * LANES] = r[:, LANES:]
        wd_s[...] = wd_ref[0, 0].astype(BF16)

    @pl.when(i < nb)
    def _():
        for r in range(bm):
            _tile_copy(h2_hbm, 0, xbuf.at[slot], 0, sems.at[slot]).wait()
        x = _to_rows(xbuf.at[slot], bm)
        gu = jnp.dot(x.astype(BF16), wgu_s[...], preferred_element_type=F32) + bgu_ref[0]
        gate = jnp.minimum(gu[:, :D_FF], SWIGLU_LIMIT)
        up = jnp.clip(gu[:, D_FF:], -SWIGLU_LIMIT, SWIGLU_LIMIT)
        hdn = (up + 1.0) * (gate * _sigmoid(SWIGLU_ALPHA * gate))
        _store_rows(y_ref, jnp.dot(hdn.astype(BF16), wd_s[...], preferred_element_type=F32) + bd_ref[0])

    @pl.when(i >= nb)
    def _():
        y_ref[...] = jnp.zeros_like(y_ref)


def _experts(layer, blk_e, nblk, tok3, h2t, w_gu, b_gu_r, w_down, b_down):
    n_blocks = tok3.shape[0]
    bm = EXPERT_BLOCK
    d = D_MODEL
    n_e, n_gu = w_gu.shape[1], w_gu.shape[3]
    w4 = lambda i, be, nb: (layer, be[i], 0, 0)
    b3 = lambda i, be, nb: (be[i], 0, 0)
    grid_spec = pltpu.PrefetchScalarGridSpec(
        num_scalar_prefetch=2,
        grid=(n_blocks,),
        in_specs=[pl.BlockSpec((1, 1, bm), lambda i, be, nb: (0, 0, 0)),
                  pl.BlockSpec((1, 1, bm), lambda i, be, nb: (jnp.minimum(i + 1, n_blocks - 1), 0, 0)),
                  pl.BlockSpec(memory_space=pl.ANY),
                  pl.BlockSpec((1, 1, d, n_gu), w4), pl.BlockSpec((1, 1, n_gu), b3),
                  pl.BlockSpec((1, 1, D_FF, d), w4), pl.BlockSpec((1, 1, d), b3),
                  pl.BlockSpec((2 * LANES, 2 * LANES), lambda i, be, nb: (0, 0))],
        out_specs=pl.BlockSpec((bm * ROW_TILES, LANES), lambda i, be, nb: (i, 0)),
        scratch_shapes=[pltpu.VMEM((d, n_gu), BF16), pltpu.VMEM((D_FF, d), BF16),
                        pltpu.VMEM((2, bm * ROW_TILES, LANES), F32), pltpu.SMEM((bm,), jnp.int32),
                        pltpu.SemaphoreType.DMA, pltpu.SemaphoreType.DMA((2,))],
    )
    return pl.pallas_call(
        _expert_kernel,
        grid_spec=grid_spec,
        out_shape=jax.ShapeDtypeStruct((n_blocks * bm * ROW_TILES, LANES), F32),
        compiler_params=_params("arbitrary"),
        name="moe_experts",
    )(blk_e, nblk, tok3, tok3, h2t, w_gu, b_gu_r.reshape(n_e, 1, n_gu), w_down, b_down.reshape(n_e, 1, d),
      _deinterleave_matrix())


def _combine_kernel(d0_ref, dn_ref, x1_ref, route_ref, mod_ref, gf_ref, yb_hbm, o_ref, idx_smem, rows, idx_sem,
                    sems, *, final, ntiles):
    i = pl.program_id(0)
    tm = x1_ref.shape[0]
    n = tm * TOP_K
    slot = lax.rem(i, 2)

    def gather(dest_ref, into):
        _load_indices(dest_ref, idx_smem, idx_sem)
        for j in range(n):
            _tile_copy(yb_hbm, idx_smem[j], rows.at[into, j % TOP_K], (j // TOP_K) * SUBLANES,
                       sems.at[into]).start()

    @pl.when(i == 0)
    def _():
        gather(d0_ref, 0)

    @pl.when(i + 1 < ntiles)
    def _():
        gather(dn_ref, 1 - slot)

    for j in range(n):
        _tile_copy(yb_hbm, 0, rows.at[slot, 0], 0, sems.at[slot]).wait()
    route = route_ref[...]
    f = _to_rows(rows.at[slot, 0], tm) * route[:, TOP_K:TOP_K + 1]
    for kk in range(1, TOP_K):
        f = f + _to_rows(rows.at[slot, kk], tm) * route[:, TOP_K + kk:TOP_K + kk + 1]
    x2 = x1_ref[...] + mod_ref[0, 5:6, :] * f
    o_ref[...] = _rms(x2) * gf_ref[...] if final else x2


def _combine(dest3, x1, route, modtab, g_final, yb, group_of_tile, final):
    t, d = x1.shape
    tm = TOKEN_TILE
    ntiles = t // tm
    n = dest3.shape[-1]
    row = lambda i: (i, 0)
    return pl.pallas_call(
        functools.partial(_combine_kernel, final=final, ntiles=ntiles),
        grid=(ntiles,),
        in_specs=[pl.BlockSpec((1, 1, n), lambda i: (0, 0, 0)),
                  pl.BlockSpec((1, 1, n), lambda i: (jnp.minimum(i + 1, ntiles - 1), 0, 0)),
                  pl.BlockSpec((tm, d), row), pl.BlockSpec((tm, LANES), row),
                  pl.BlockSpec((1, 8, d), lambda i: (group_of_tile(i), 0, 0)),
                  pl.BlockSpec((1, d), lambda i: (0, 0)),
                  pl.BlockSpec(memory_space=pl.ANY)],
        out_specs=pl.BlockSpec((tm, d), row),
        out_shape=jax.ShapeDtypeStruct((t, d), F32),
        scratch_shapes=[pltpu.SMEM((n,), jnp.int32), pltpu.VMEM((2, TOP_K, tm * ROW_TILES, LANES), F32),
                        pltpu.SemaphoreType.DMA, pltpu.SemaphoreType.DMA((2,))],
        compiler_params=_params("arbitrary"),
        name="moe_combine",
    )(dest3, dest3, x1, route, modtab, g_final.reshape(1, d), yb)


def _grid_swap(x, r1, r2):
    n, d = x.shape
    b = n // (r1 * r2)
    return x.reshape(b, r1, r2, d).swapaxes(1, 2).reshape(n, d)


def _prep_w_in(w_in):
    sizes = (QA_W, QA_W, VA_W, GATE_RANK, VA_W, QKVB_W, 2 * H_B, 2 * H_B, VB_W)
    offs = np.concatenate([[0], np.cumsum(sizes)])
    qa, ka, va, glr, oga, qkvb, ab, bb, zb = [w_in[..., offs[i]:offs[i + 1]] for i in range(len(sizes))]
    pad = jnp.zeros(w_in.shape[:-1] + (SM_W - GATE_RANK - 4 * H_B,), w_in.dtype)
    return jnp.concatenate([qa, ka, va, oga, zb, qkvb, glr, ab, bb, pad], axis=-1).astype(BF16)


def kernel(x_prompt, x_sample, state_gla, state_gdn, c, c_ctx, w_mod, b_mod, g_norm1, g_norm2, w_in,
           gla_w_gate2, gla_b_gate, gla_g_onorm, gdn_conv_w, gdn_a_log, gdn_dt_bias, gdn_g_onorm, w_out,
           router_w, router_b, w_gu, b_gu, w_down, b_down, g_final):
    bp, sp, d = x_prompt.shape
    bs, ns, _ = x_sample.shape
    depth = w_mod.shape[0]
    n_p, n_s = bp * sp, bs * ns
    t = n_p + n_s
    tm = TOKEN_TILE
    nsp, nss = SEQS_PER_STEP_SHORT, SEQS_PER_STEP_LONG
    assert d == D_MODEL and sp % tm == 0 and ns % tm == 0 and bs + 1 <= 8
    assert bp % nsp == 0 and bs % nss == 0 and n_p % (nss * ns) == 0
    rows_s = ns // GRID_W

    def group_of_tile(i):
        return jnp.where(i < n_p // tm, 0, 1 + (i - n_p // tm) // (ns // tm))

    cond = jnp.concatenate([c_ctx[None], c, jnp.zeros((8 - 1 - bs, d), F32)], axis=0)
    mods = _modulation(cond, w_mod, b_mod)
    mods = mods.reshape(depth, 8, N_MOD, d)[:, :1 + bs]
    modtab = jnp.concatenate([mods, jnp.zeros((depth, 1 + bs, 8 - N_MOD, d), F32)], axis=2)

    w_in_r = _prep_w_in(w_in)
    w_out_b = w_out.astype(BF16)
    wg = jnp.concatenate([gla_w_gate2, jnp.zeros((depth, 2, SM_W - GATE_RANK, QA_W), F32)], axis=2)
    wgh = wg.astype(BF16)
    wgl = (wg - wgh.astype(F32)).astype(BF16)
    bg = gla_b_gate.reshape(depth, 2, 1, QA_W)
    lane_pad = lambda v, off: jnp.pad(v.reshape(depth, 1, -1), ((0, 0), (0, 0), (off, SM_W - off - v[0].size)))
    a_log_vec = lane_pad(gdn_a_log, SM_A0)
    dtb_vec = lane_pad(gdn_dt_bias, SM_A0)
    conv_w = jnp.pad(gdn_conv_w, ((0, 0), (0, 8 - gdn_conv_w.shape[1]), (0, 0)))
    gon = jnp.stack([gla_g_onorm, gdn_g_onorm], axis=1)
    rw = jnp.pad(router_w, ((0, 0), (0, 0), (0, LANES - N_EXPERTS)))
    rb = jnp.pad(router_b, ((0, 0), (0, LANES - N_EXPERTS)), constant_values=-1e30).reshape(depth, 1, LANES)
    b_gu_r = jnp.concatenate([b_gu[..., 0::2], b_gu[..., 1::2]], axis=-1)

    hsel = jnp.eye(H_A, dtype=F32)
    st_a_s = jnp.einsum("bldhkv,hg->bldhvgk", state_gla.astype(F32), hsel).reshape(bs, depth, 2, VA_W, QA_W)
    st_b_s = state_gdn.astype(F32)
    st_a_0 = jnp.zeros((nsp, 2, VA_W, QA_W), F32)
    st_b_0 = jnp.zeros((nsp, 2, H_B, DK_B, DV_B), F32)

    n_blocks = (t * TOP_K) // EXPERT_BLOCK + N_EXPERTS
    cap = n_blocks * EXPERT_BLOCK
    tok_of_slot = jnp.repeat(jnp.arange(t, dtype=jnp.int32) * ROW_TILES, TOP_K)

    x = jnp.concatenate([x_prompt.reshape(n_p, d), x_sample.reshape(n_s, d)], axis=0)
    new_gla, new_gdn = [], []
    for l in range(depth):
        if l % 2 == 1:
            x = jnp.concatenate([x[:n_p], _grid_swap(x[n_p:], rows_s, GRID_W)], axis=0)
        pa, gz, pb, sm = _pre_mixer(x, modtab[l], g_norm1[l], w_in_r[l], group_of_tile)
        oa_p, sa = _gla(pa, sm, st_a_0, wgh[l], wgl[l], bg[l], ntotal=bp, seqlen=sp, nseq=nsp, row_block0=0,
                        shared_state=True)
        oa_s, _ = _gla(pa, sm, st_a_s[:, l], wgh[l], wgl[l], bg[l], ntotal=bs, seqlen=ns, nseq=nss,
                       row_block0=n_p // (nss * ns), shared_state=False)
        ob_p, sb = _gdn(pb, sm, st_b_0, conv_w[l], a_log_vec[l], dtb_vec[l], ntotal=bp, seqlen=sp, nseq=nsp,
                        row_block0=0, shared_state=True)
        ob_s, _ = _gdn(pb, sm, st_b_s[:, l], conv_w[l], a_log_vec[l], dtb_vec[l], ntotal=bs, seqlen=ns, nseq=nss,
                       row_block0=n_p // (nss * ns), shared_state=False)
        new_gla.append(sa)
        new_gdn.append(sb)
        oa = jnp.concatenate([oa_p, oa_s], axis=0)
        ob = jnp.concatenate([ob_p, ob_s], axis=0)
        x1, h2t, route, counts = _post_mixer(x, oa, ob, gz, modtab[l], gon[l], w_out_b[l], g_norm2[l],
                                             rw[l], rb[l], group_of_tile)

        top_i = route[:, :TOP_K].astype(jnp.int32)
        rank = route[:, 2 * TOP_K:3 * TOP_K].astype(jnp.int32)
        sizes = counts[0, :N_EXPERTS].astype(jnp.int32)
        padded = (sizes + EXPERT_BLOCK - 1) // EXPERT_BLOCK * EXPERT_BLOCK
        pend = jnp.cumsum(padded)
        pstart = pend - padded
        onehot = top_i[..., None] == jnp.arange(N_EXPERTS, dtype=jnp.int32)
        dest = (rank + jnp.sum(jnp.where(onehot, pstart, 0), axis=-1)).reshape(-1)
        dest3 = (dest * ROW_TILES).reshape(t // tm, 1, tm * TOP_K)
        tok3 = jnp.zeros((cap,), jnp.int32).at[dest].set(tok_of_slot, unique_indices=True)
        tok3 = tok3.reshape(n_blocks, 1, EXPERT_BLOCK)
        nblk = (pend[-1] // EXPERT_BLOCK).reshape(1)
        blk_start = jnp.minimum(jnp.arange(n_blocks, dtype=jnp.int32), nblk[0] - 1) * EXPERT_BLOCK
        blk_e = jnp.sum(blk_start[:, None] >= pend[None, :], axis=-1).astype(jnp.int32)

        yb = _experts(l, blk_e, nblk, tok3, h2t, w_gu, b_gu_r[l], w_down, b_down[l])
        x = _combine(dest3, x1, route, modtab[l], g_final, yb, group_of_tile, final=(l == depth - 1))
        if l % 2 == 1:
            x = jnp.concatenate([x[:n_p], _grid_swap(x[n_p:], GRID_W, rows_s)], axis=0)

    y_prompt = x[:n_p].reshape(bp, sp, d)
    y_sample = x[n_p:].reshape(bs, ns, d)
    sa_all = jnp.stack(new_gla, axis=1).reshape(bp, depth, 2, H_A, DV_A, H_A, DK_A)
    new_state_gla = jnp.einsum("bldhvgk,hg->bldhkv", sa_all, hsel)
    new_state_gdn = jnp.stack(new_gdn, axis=1)
    return (y_prompt, y_sample, new_state_gla, new_state_gdn)
```

```python
import functools

import numpy as np
import jax
import jax.numpy as jnp
from jax import lax
from jax.experimental import pallas as pl
from jax.experimental.pallas import tpu as pltpu

F32 = jnp.float32
BF16 = jnp.bfloat16

D_MODEL = 1024
GRID_W = 64
H_A, DK_A, DV_A = 4, 64, 128
H_B, DK_B, DV_B = 4, 128, 128
GATE_RANK = 16
GATE_NORMALIZER = 16.0
CHUNK = 64
N_EXPERTS = 32
TOP_K = 4
D_FF = 1024
SWIGLU_LIMIT = 7.0
SWIGLU_ALPHA = 1.702
N_MOD = 6
EPS = 1e-6

QA_W = H_A * DK_A
VA_W = H_A * DV_A
QB_W = H_B * DK_B
VB_W = H_B * DV_B
QKVB_W = 2 * QB_W + VB_W
PA_W = 2 * QA_W + VA_W
GZ_W = VA_W + VB_W
SM_W = 128
SM_A0 = GATE_RANK
SM_B0 = GATE_RANK + 2 * H_B
W_IN_COLS = PA_W + GZ_W + QKVB_W + SM_W

TOKEN_TILE = 256
EXPERT_BLOCK = 256
EXPERT_SLOTS = 3
LANES = 128
SUBLANES = 8
ROW_TILES = D_MODEL // LANES
BF16_ROWS = 16
VMEM_LIMIT_BYTES = 56 * 2**20
SEQS_PER_STEP_SHORT = 4
SEQS_PER_STEP_LONG = 2

_NT = (((1,), (1,)), ((), ()))
_TN = (((0,), (0,)), ((), ()))

assert ROW_TILES == SUBLANES


def _params(*sem):
    return pltpu.CompilerParams(dimension_semantics=sem, vmem_limit_bytes=VMEM_LIMIT_BYTES)


def _dot(a, b):
    return jnp.dot(a.astype(BF16), b.astype(BF16), preferred_element_type=F32)


def _dot_nt(a, b):
    return lax.dot_general(a.astype(BF16), b.astype(BF16), _NT, preferred_element_type=F32)


def _dot_tn(a, b):
    return lax.dot_general(a.astype(BF16), b.astype(BF16), _TN, preferred_element_type=F32)


def _split(x):
    hi = x.astype(BF16)
    lo = (x - hi.astype(F32)).astype(BF16)
    return hi, lo


def _dot_x2(m, x):
    hi, lo = _split(x)
    return _dot(m, hi) + _dot(m, lo)


def _dot_2x(x, m):
    hi, lo = _split(x)
    return _dot(hi, m) + _dot(lo, m)


def _dot3(a, b):
    ah, al = _split(a)
    bh, bl = _split(b)
    return _dot(ah, bh) + _dot(ah, bl) + _dot(al, bh)


def _sigmoid(x):
    return 1.0 / (1.0 + jnp.exp(-x))


def _softplus(x):
    return jnp.maximum(x, 0.0) + jnp.log(1.0 + jnp.exp(-jnp.abs(x)))


def _rms(x):
    return x * lax.rsqrt(jnp.mean(x * x, axis=-1, keepdims=True) + EPS)


def _stack4(x, mask):
    xb = x.astype(BF16)
    return jnp.concatenate([xb, xb, xb, xb], axis=0) * mask


def _to_rows(tiles_ref, n):
    return jnp.concatenate([tiles_ref[pl.ds(j, n, stride=ROW_TILES), :] for j in range(ROW_TILES)], axis=1)


def _store_rows(tiles_ref, x):
    n = x.shape[0]
    for j in range(ROW_TILES):
        tiles_ref[pl.ds(j, n, stride=ROW_TILES), :] = x[:, j * LANES:(j + 1) * LANES]


def _block_mask(rows, row_blk, cols, col_blk):
    r = np.arange(rows)[:, None] // row_blk
    c = np.arange(cols)[None, :] // col_blk
    return (r == c).astype(np.float32)


@functools.lru_cache(maxsize=None)
def _gla_consts():
    c = CHUNK
    w = np.zeros((7, c, c), np.float32)
    m = np.zeros((7, c, c), np.float32)
    w[0] = np.tril(np.ones((c, c)))
    m[0] = np.eye(c)
    i = np.arange(c)
    for l, s in enumerate((32, 16, 8, 4, 2, 1), start=1):
        ref = (i // (2 * s)) * 2 * s + s - 1
        for r in range(c):
            if r > ref[r]:
                w[l, r, ref[r] + 1:r + 1] = 1.0
            else:
                w[l, r, r + 1:ref[r] + 1] = 1.0
        same = (i[:, None] // (2 * s)) == (i[None, :] // (2 * s))
        m[l] = same & ((i[:, None] % (2 * s)) >= s) & ((i[None, :] % (2 * s)) < s)
    assert np.array_equal(m.sum(0), np.tril(np.ones((c, c))))
    wall = np.stack([w.reshape(7 * c, c), w[:, ::-1, ::-1].reshape(7 * c, c)])
    masks = np.stack([np.tile(m, (1, 1, H_A)), np.tile(m[:, ::-1, ::-1], (1, 1, H_A))])
    return dict(
        wall=jnp.asarray(wall, BF16),
        masks=jnp.asarray(masks, F32),
        hm=jnp.asarray(_block_mask(H_A * c, c, QA_W, DK_A), BF16),
        vm=jnp.asarray(_block_mask(H_A * c, c, VA_W, DV_A), BF16),
        bd=jnp.asarray(_block_mask(VA_W, DV_A, QA_W, DK_A), F32),
    )


@functools.lru_cache(maxsize=None)
def _gdn_consts():
    c = CHUNK
    tril = np.tril(np.ones((c, c), np.float32))
    lc = np.stack([tril, tril.T])
    ut = np.stack([np.tile(lc[d].T, (1, H_B)) for d in range(2)])
    cm = np.stack([np.tile(lc[d], (1, H_B)) for d in range(2)])
    eye = np.tile(np.eye(c, dtype=np.float32), (1, H_B))
    sm = cm - eye[None]
    i = np.arange(c)
    dblk = np.tile(((i[:, None] // 16) == (i[None, :] // 16)).astype(np.float32), (1, H_B))
    eg = np.zeros((2, SM_W, H_B * c), np.float32)
    eb = np.zeros((2, SM_W, H_B * c), np.float32)
    egw = np.zeros((2, SM_W, QB_W), np.float32)
    ebw = np.zeros((2, SM_W, QB_W), np.float32)
    for d in range(2):
        for h in range(H_B):
            eg[d, SM_A0 + d * H_B + h, h * c:(h + 1) * c] = 1.0
            eb[d, SM_B0 + d * H_B + h, h * c:(h + 1) * c] = 1.0
            egw[d, SM_A0 + d * H_B + h, h * DK_B:(h + 1) * DK_B] = 1.0
            ebw[d, SM_B0 + d * H_B + h, h * DK_B:(h + 1) * DK_B] = 1.0
    return dict(
        lc=jnp.asarray(lc, BF16), ut=jnp.asarray(ut, F32), cm=jnp.asarray(cm, F32), sm=jnp.asarray(sm, F32),
        eye=jnp.asarray(eye, F32), dblk=jnp.asarray(dblk, F32),
        eg=jnp.asarray(eg, BF16), eb=jnp.asarray(eb, BF16), egw=jnp.asarray(egw, BF16), ebw=jnp.asarray(ebw, BF16),
        hm=jnp.asarray(_block_mask(H_B * c, c, H_B * c, c), BF16),
        km=jnp.asarray(_block_mask(H_B * c, c, QB_W, DK_B), BF16),
    )


@functools.lru_cache(maxsize=None)
def _deinterleave_matrix():
    p = np.zeros((2 * LANES, 2 * LANES), np.float32)
    j = np.arange(LANES)
    p[2 * j, j] = 1.0
    p[2 * j + 1, LANES + j] = 1.0
    return jnp.asarray(p, BF16)


def _mod_kernel(c_ref, w_ref, b_ref, o_ref):
    c = c_ref[...]
    o_ref[0] = _dot3(c * _sigmoid(c), w_ref[0]) + b_ref[0]


def _modulation(cond, w_mod, b_mod):
    depth, d, n = w_mod.shape
    tn = 1536
    return pl.pallas_call(
        _mod_kernel,
        grid=(depth, n // tn),
        in_specs=[pl.BlockSpec((8, d), lambda l, j: (0, 0)),
                  pl.BlockSpec((1, d, tn), lambda l, j: (l, 0, j)),
                  pl.BlockSpec((1, 1, tn), lambda l, j: (l, 0, j))],
        out_specs=pl.BlockSpec((1, 8, tn), lambda l, j: (l, 0, j)),
        out_shape=jax.ShapeDtypeStruct((depth, 8, n), F32),
        compiler_params=_params("parallel", "parallel"),
        name="modulation",
    )(cond, w_mod, b_mod.reshape(depth, 1, n))


def _pre_kernel(x_ref, mod_ref, g_ref, w_ref, pa_ref, gz_ref, pb_ref, sm_ref):
    h = _rms(x_ref[...]) * g_ref[...]
    h = h * (1.0 + mod_ref[0, 1:2, :]) + mod_ref[0, 0:1, :]
    p = jnp.dot(h.astype(BF16), w_ref[...], preferred_element_type=F32)
    pa_ref[...] = p[:, :PA_W].astype(BF16)
    gz_ref[...] = p[:, PA_W:PA_W + GZ_W].astype(BF16)
    pb_ref[...] = p[:, PA_W + GZ_W:PA_W + GZ_W + QKVB_W].astype(BF16)
    sm_ref[...] = p[:, PA_W + GZ_W + QKVB_W:]


def _pre_mixer(x, modtab, g1, w_in_r, group_of_tile):
    t, d = x.shape
    tm = TOKEN_TILE
    row = lambda i: (i, 0)
    return pl.pallas_call(
        _pre_kernel,
        grid=(t // tm,),
        in_specs=[pl.BlockSpec((tm, d), row),
                  pl.BlockSpec((1, 8, d), lambda i: (group_of_tile(i), 0, 0)),
                  pl.BlockSpec((1, d), lambda i: (0, 0)),
                  pl.BlockSpec((d, W_IN_COLS), lambda i: (0, 0))],
        out_specs=[pl.BlockSpec((tm, PA_W), row), pl.BlockSpec((tm, GZ_W), row),
                   pl.BlockSpec((tm, QKVB_W), row), pl.BlockSpec((tm, SM_W), row)],
        out_shape=[jax.ShapeDtypeStruct((t, PA_W), BF16), jax.ShapeDtypeStruct((t, GZ_W), BF16),
                   jax.ShapeDtypeStruct((t, QKVB_W), BF16), jax.ShapeDtypeStruct((t, SM_W), F32)],
        compiler_params=_params("parallel"),
        name="pre_mixer",
    )(x, modtab, g1.reshape(1, d), w_in_r)


def _gla_kernel(pa_ref, sm_ref, wall_ref, mask_ref, wgh_ref, wgl_ref, bg_ref, hm_ref, vm_ref, bd_ref, st0_ref,
                o_ref, stf_ref, st_scr, *, nchunk, nseq):
    c = CHUNK
    seqlen = nchunk * c
    hm = hm_ref[...]
    vm = vm_ref[...]
    bd = bd_ref[...]
    st_scr[...] = st0_ref[...]

    chains = [(s, d) for s in range(nseq) for d in range(2)]
    dirs = [d for _, d in chains]

    def body(ci, carry):
        rows = [pl.ds(pl.multiple_of(s * seqlen + (ci if d == 0 else nchunk - 1 - ci) * c, c), c)
                for s, d in chains]
        q = [pa_ref[r, 0:QA_W].astype(F32) * (DK_A ** -0.5) for r in rows]
        k = [pa_ref[r, QA_W:2 * QA_W].astype(F32) for r in rows]
        v = [pa_ref[r, 2 * QA_W:PA_W] for r in rows]
        sm = [_split(sm_ref[r, :]) for r in rows]
        x = _each(lambda y, d: _dot(y[0], wgh_ref[d]) + _dot(y[0], wgl_ref[d]) + _dot(y[1], wgh_ref[d]) + bg_ref[d],
                  sm, dirs)
        gk = [(jnp.minimum(y, 0.0) - jnp.log(1.0 + jnp.exp(-jnp.abs(y)))) * (1.0 / GATE_NORMALIZER) for y in x]
        e_all = _each(lambda g, d: _dot_x2(wall_ref[d], g), gk, dirs)
        a = None
        for l in range(7):
            if l == 0:
                qt, kt = q, k
            else:
                f = [jnp.exp(e[l * c:(l + 1) * c]) for e in e_all]
                qt = _each(lambda y, z: y * z, q, f)
                kt = _each(lambda y, z: y * z, k, f)
            r = _each(lambda y, z, d: _dot_nt(y, _stack4(z, hm)) * mask_ref[d, l], qt, kt, dirs)
            a = r if a is None else _each(lambda y, z: y + z, a, r)
        b = [e[0:c] for e in e_all]
        o = _each(lambda aa, vv, qq, bb, sd: _dot(aa, _stack4(vv, vm)) + _dot_nt(qq * jnp.exp(bb), st_scr[sd]),
                  a, v, q, b, chains)
        for r, d, y in zip(rows, dirs, o):
            o_ref[r, d * VA_W:(d + 1) * VA_W] = y
        b_last = _each(lambda bb, d: bb[c - 1:c] if d == 0 else bb[0:1], b, dirs)
        upd = _each(lambda vv, kk, bl, bb: _dot_tn(vv, kk * jnp.exp(bl - bb)), v, k, b_last, b)
        for sd, bl, up in zip(chains, b_last, upd):
            st_scr[sd] = st_scr[sd] * jnp.exp(bl) + up * bd
        return carry

    lax.fori_loop(0, nchunk, body, 0)
    stf_ref[...] = st_scr[...]


def _gla(pa, sm, st0, wgh, wgl, bg, *, ntotal, seqlen, nseq, row_block0, shared_state):
    cst = _gla_consts()
    nchunk = seqlen // CHUNK
    rows = nseq * seqlen
    seq = lambda b: (row_block0 + b, 0)
    full = lambda *shape: pl.BlockSpec(shape, lambda b: (0,) * len(shape))
    st_in = (lambda b: (0, 0, 0, 0)) if shared_state else (lambda b: (b, 0, 0, 0))
    return pl.pallas_call(
        functools.partial(_gla_kernel, nchunk=nchunk, nseq=nseq),
        grid=(ntotal // nseq,),
        in_specs=[pl.BlockSpec((rows, PA_W), seq), pl.BlockSpec((rows, SM_W), seq),
                  full(2, 7 * CHUNK, CHUNK), full(2, 7, CHUNK, QA_W),
                  full(2, SM_W, QA_W), full(2, SM_W, QA_W), full(2, 1, QA_W),
                  full(QA_W, QA_W), full(QA_W, VA_W), full(VA_W, QA_W),
                  pl.BlockSpec((nseq, 2, VA_W, QA_W), st_in)],
        out_specs=[pl.BlockSpec((rows, 2 * VA_W), lambda b: (b, 0)),
                   pl.BlockSpec((nseq, 2, VA_W, QA_W), lambda b: (b, 0, 0, 0))],
        out_shape=[jax.ShapeDtypeStruct((ntotal * seqlen, 2 * VA_W), F32),
                   jax.ShapeDtypeStruct((ntotal, 2, VA_W, QA_W), F32)],
        scratch_shapes=[pltpu.VMEM((nseq, 2, VA_W, QA_W), F32)],
        compiler_params=_params("parallel"),
        name="gla_l%d" % seqlen,
    )(pa, sm, cst["wall"], cst["masks"], wgh, wgl, bg, cst["hm"], cst["vm"], cst["bd"], st0)


def _each(f, *lists):
    return [f(*xs) for xs in zip(*lists)]


def _mm_heads(xs, ys, hm):
    return _each(lambda x, y: _dot(x, _stack4(y, hm)), xs, ys)


def _unit_tri_inverse(ms, eye, dblk, hm):
    mm = functools.partial(_mm_heads, hm=hm)
    dg = [m * dblk for m in ms]
    lo = _each(lambda m, g: m - g, ms, dg)
    d2 = mm(dg, dg)
    d3 = mm(dg, d2)
    d4 = mm(d2, d2)
    p1 = _each(lambda g, a, b: eye - g + a - b, dg, d2, d3)
    p2 = _each(lambda a, b: a + b, p1, mm(p1, d4))
    d8 = mm(d4, d4)
    td = _each(lambda a, b: a + b, p2, mm(p2, d8))
    e1 = mm(td, lo)
    e2 = mm(e1, e1)
    e3 = mm(e1, e2)
    return mm(_each(lambda a, b, c: eye - a + b - c, e1, e2, e3), td)


def _gdn_kernel(pb_ref, sm_ref, cw_ref, al_ref, dtb_ref, lc_ref, ut_ref, cm_ref, smk_ref, eye_ref, dblk_ref,
                eg_ref, eb_ref, egw_ref, ebw_ref, hm_ref, km_ref, st0_ref, o_ref, stf_ref, qkv_scr, s_scr,
                *, nchunk, nseq):
    c = CHUNK
    seqlen = nchunk * c
    w0 = cw_ref[0:1, :]
    w1 = cw_ref[1:2, :]
    w2 = cw_ref[2:3, :]

    def conv_body(cc, carry):
        for s in range(nseq):
            r0 = pl.multiple_of(s * seqlen + cc * c, c)
            x = pb_ref[pl.ds(r0, c), :].astype(F32)
            p0 = pl.multiple_of(jnp.maximum(r0 - BF16_ROWS, 0), BF16_ROWS)
            n0 = pl.multiple_of(jnp.minimum(r0 + c, nseq * seqlen - BF16_ROWS), BF16_ROWS)
            pg = pb_ref[pl.ds(p0, BF16_ROWS), :]
            ng = pb_ref[pl.ds(n0, BF16_ROWS), :]
            prow = jnp.where(cc > 0, pg.astype(F32)[BF16_ROWS - 1:BF16_ROWS], 0.0)
            nrow = jnp.where(cc < nchunk - 1, ng.astype(F32)[0:1], 0.0)
            rid = lax.broadcasted_iota(jnp.int32, x.shape, 0)
            xp = jnp.where(rid == 0, prow, pltpu.roll(x, 1, axis=0))
            xn = jnp.where(rid == c - 1, nrow, pltpu.roll(x, c - 1, axis=0))
            y = w0 * xp + w1 * x + w2 * xn
            sl = y * _sigmoid(y)
            parts = []
            for h in range(2 * H_B):
                p = sl[:, h * DK_B:(h + 1) * DK_B]
                nrm = lax.rsqrt(jnp.sum(p * p, axis=-1, keepdims=True) + EPS)
                parts.append(p * (nrm * (DK_B ** -0.5 if h < H_B else 1.0)))
            parts.append(sl[:, 2 * QB_W:])
            qkv_scr[pl.ds(r0, c), :] = jnp.concatenate(parts, axis=1).astype(BF16)
        return carry

    lax.fori_loop(0, nchunk, conv_body, 0)

    hm = hm_ref[...]
    km = km_ref[...]
    eye = eye_ref[...]
    dblk = dblk_ref[...]
    ones = jnp.ones((c, c), BF16)
    neg_a = -jnp.exp(al_ref[...])
    dtb = dtb_ref[...]
    s_scr[...] = st0_ref[...]

    chains = [(s, d) for s in range(nseq) for d in range(2)]
    dirs = [d for _, d in chains]
    heads = [slice(h * DK_B, (h + 1) * DK_B) for h in range(H_B)]

    def body(ci, carry):
        rows = [pl.ds(pl.multiple_of(s * seqlen + (ci if d == 0 else nchunk - 1 - ci) * c, c), c)
                for s, d in chains]
        q = [qkv_scr[r, 0:QB_W].astype(F32) for r in rows]
        k = [qkv_scr[r, QB_W:2 * QB_W].astype(F32) for r in rows]
        v = [qkv_scr[r, 2 * QB_W:QKVB_W].astype(F32) for r in rows]
        sm = [sm_ref[r, :] for r in rows]
        g_all = [_split(neg_a * _softplus(x + dtb)) for x in sm]
        beta_all = [_split(_sigmoid(x)) for x in sm]
        two = lambda ref: (lambda x, d: _dot(x[0], ref[d]) + _dot(x[1], ref[d]))
        g_b = _each(two(eg_ref), g_all, dirs)
        g_w = _each(two(egw_ref), g_all, dirs)
        beta_i = _each(two(eb_ref), beta_all, dirs)
        beta_w = _each(two(ebw_ref), beta_all, dirs)
        gam_i = _each(lambda x, d: _dot_x2(lc_ref[d], x), g_b, dirs)
        gam_j = _each(lambda x, d: _dot_x2(ones, x * ut_ref[d]), g_b, dirs)
        gam_w = _each(lambda x, d: _dot_x2(lc_ref[d], x), g_w, dirs)
        dec = _each(lambda a, b, d: jnp.exp(jnp.minimum(a - b, 0.0)) * cm_ref[d], gam_i, gam_j, dirs)
        kkqk = _each(lambda kk, qq: _dot_nt(jnp.concatenate([kk, qq], axis=0), _stack4(kk, km)), k, q)
        m = _each(lambda bi, x, dc, d: smk_ref[d] * bi * x[:c] * dc, beta_i, kkqk, dec, dirs)
        t_inv = _unit_tri_inverse(m, eye, dblk, hm)
        eg_w = [jnp.exp(x) for x in gam_w]
        kh = _each(lambda kk, bw, e: kk * (bw * e), k, beta_w, eg_w)
        qh = _each(lambda qq, e: qq * e, q, eg_w)
        res = [[_dot(jnp.concatenate([a[:, hs], b[:, hs]], axis=0), s_scr[s, d, h])
                for h, hs in enumerate(heads)] for a, b, (s, d) in zip(kh, qh, chains)]
        rhs = _each(lambda vv, bw, r: vv * bw - jnp.concatenate([x[:c] for x in r], axis=1), v, beta_w, res)
        u = _each(lambda t, r: _dot(t, _stack4(r, km)), t_inv, rhs)
        o = _each(lambda r, x, dc, uu: jnp.concatenate([y[c:] for y in r], axis=1)
                  + _dot(x[c:] * dc, _stack4(uu, km)), res, kkqk, dec, u)
        for r, d, x in zip(rows, dirs, o):
            o_ref[r, d * VB_W:(d + 1) * VB_W] = x
        g_last = _each(lambda x, d: x[c - 1:c] if d == 0 else x[0:1], gam_w, dirs)
        kd = _each(lambda kk, gl, gw: kk * jnp.exp(gl - gw), k, g_last, gam_w)
        upd = [[_dot_tn(a[:, hs], b[:, hs]) for hs in heads] for a, b in zip(kd, u)]
        for (s, d), gl, up in zip(chains, g_last, upd):
            for h, hs in enumerate(heads):
                s_scr[s, d, h] = s_scr[s, d, h] * jnp.exp(gl[:, hs]) + up[h]
        return carry

    lax.fori_loop(0, nchunk, body, 0)
    stf_ref[...] = s_scr[...]


def _gdn(pb, sm, st0, conv_w, a_log_vec, dtb_vec, *, ntotal, seqlen, nseq, row_block0, shared_state):
    cst = _gdn_consts()
    nchunk = seqlen // CHUNK
    rows = nseq * seqlen
    hc = H_B * CHUNK
    seq = lambda b: (row_block0 + b, 0)
    full = lambda *shape: pl.BlockSpec(shape, lambda b: (0,) * len(shape))
    st_in = (lambda b: (0, 0, 0, 0, 0)) if shared_state else (lambda b: (b, 0, 0, 0, 0))
    return pl.pallas_call(
        functools.partial(_gdn_kernel, nchunk=nchunk, nseq=nseq),
        grid=(ntotal // nseq,),
        in_specs=[pl.BlockSpec((rows, QKVB_W), seq), pl.BlockSpec((rows, SM_W), seq),
                  full(8, QKVB_W), full(1, SM_W), full(1, SM_W),
                  full(2, CHUNK, CHUNK), full(2, CHUNK, hc), full(2, CHUNK, hc), full(2, CHUNK, hc),
                  full(CHUNK, hc), full(CHUNK, hc),
                  full(2, SM_W, hc), full(2, SM_W, hc), full(2, SM_W, QB_W), full(2, SM_W, QB_W),
                  full(hc, hc), full(hc, QB_W),
                  pl.BlockSpec((nseq, 2, H_B, DK_B, DV_B), st_in)],
        out_specs=[pl.BlockSpec((rows, 2 * VB_W), lambda b: (b, 0)),
                   pl.BlockSpec((nseq, 2, H_B, DK_B, DV_B), lambda b: (b, 0, 0, 0, 0))],
        out_shape=[jax.ShapeDtypeStruct((ntotal * seqlen, 2 * VB_W), F32),
                   jax.ShapeDtypeStruct((ntotal, 2, H_B, DK_B, DV_B), F32)],
        scratch_shapes=[pltpu.VMEM((rows, QKVB_W), BF16), pltpu.VMEM((nseq, 2, H_B, DK_B, DV_B), F32)],
        compiler_params=_params("parallel"),
        name="gdn_l%d" % seqlen,
    )(pb, sm, conv_w, a_log_vec, dtb_vec, cst["lc"], cst["ut"], cst["cm"], cst["sm"], cst["eye"], cst["dblk"],
      cst["eg"], cst["eb"], cst["egw"], cst["ebw"], cst["hm"], cst["km"], st0)


def _post_kernel(x_ref, oap_ref, oas_ref, obp_ref, obs_ref, gz_ref, mod_ref, gon_ref, wo_ref, g2_ref, rw_ref,
                 rb_ref, ls_ref, x1_ref, h2_ref, route_ref, cnt_ref, *, context_tiles):
    is_context = pl.program_id(0) < context_tiles

    @pl.when(pl.program_id(0) == 0)
    def _():
        cnt_ref[...] = jnp.zeros_like(cnt_ref)

    gz = gz_ref[...]
    heads = []
    for grp, (op_ref, os_ref) in enumerate(((oap_ref, oas_ref), (obp_ref, obs_ref))):
        o2 = jnp.where(is_context, op_ref[...], os_ref[...])
        o = o2[:, :VA_W] + o2[:, VA_W:]
        for h in range(H_A):
            hs = slice(h * DV_A, (h + 1) * DV_A)
            gate = gz[:, grp * VA_W + h * DV_A:grp * VA_W + (h + 1) * DV_A].astype(F32)
            heads.append(_rms(o[:, hs]) * gon_ref[grp:grp + 1, :] * (gate * _sigmoid(gate)))
    mixed = jnp.concatenate(heads, axis=1)
    y = jnp.dot(mixed.astype(BF16), wo_ref[...], preferred_element_type=F32)
    x1 = x_ref[...] + mod_ref[0, 2:3, :] * y
    x1_ref[...] = x1
    h2 = _rms(x1) * g2_ref[...]
    h2 = h2 * (1.0 + mod_ref[0, 4:5, :]) + mod_ref[0, 3:4, :]
    _store_rows(h2_ref, h2)

    logits = _dot3(h2, rw_ref[...]) + rb_ref[...]
    lane = lax.broadcasted_iota(jnp.int32, logits.shape, 1).astype(F32)
    vals, idxs = [], []
    rem = logits
    for _ in range(TOP_K):
        mx = jnp.max(rem, axis=-1, keepdims=True)
        ix = jnp.min(jnp.where(rem == mx, lane, float(LANES)), axis=-1, keepdims=True)
        vals.append(mx)
        idxs.append(ix)
        rem = jnp.where(lane == ix, -jnp.inf, rem)
    exps = [jnp.exp(v - vals[0]) for v in vals]
    inv = 1.0 / (exps[0] + exps[1] + exps[2] + exps[3])
    onehot = jnp.zeros(logits.shape, F32)
    for ix in idxs:
        onehot = onehot + (lane == ix).astype(F32)
    before = _dot(ls_ref[...], onehot) + cnt_ref[...]
    route = jnp.zeros(logits.shape, F32)
    for kk in range(TOP_K):
        rank = jnp.sum(jnp.where(lane == idxs[kk], before, 0.0), axis=-1, keepdims=True)
        route = jnp.where(lane == kk, idxs[kk], route)
        route = jnp.where(lane == TOP_K + kk, exps[kk] * inv, route)
        route = jnp.where(lane == 2 * TOP_K + kk, rank, route)
    route_ref[...] = route
    cnt_ref[...] = cnt_ref[...] + jnp.sum(onehot, axis=0, keepdims=True)


def _post_mixer(x, oa_p, oa_s, ob_p, ob_s, gz, modtab, gon, w_out, g2, rw, rb, group_of_tile):
    t, d = x.shape
    tm = TOKEN_TILE
    context_tiles = oa_p.shape[0] // tm
    row = lambda i: (i, 0)
    ctx_row = lambda i: (jnp.minimum(i, context_tiles - 1), 0)
    lat_row = lambda i: (jnp.maximum(i - context_tiles, 0), 0)
    const = lambda i: (0, 0)
    lstrict = jnp.asarray(np.tril(np.ones((tm, tm), np.float32), -1), BF16)
    return pl.pallas_call(
        functools.partial(_post_kernel, context_tiles=context_tiles),
        grid=(t // tm,),
        in_specs=[pl.BlockSpec((tm, d), row),
                  pl.BlockSpec((tm, 2 * VA_W), ctx_row), pl.BlockSpec((tm, 2 * VA_W), lat_row),
                  pl.BlockSpec((tm, 2 * VB_W), ctx_row), pl.BlockSpec((tm, 2 * VB_W), lat_row),
                  pl.BlockSpec((tm, GZ_W), row),
                  pl.BlockSpec((1, 8, d), lambda i: (group_of_tile(i), 0, 0)),
                  pl.BlockSpec((2, DV_A), const), pl.BlockSpec((d, d), const), pl.BlockSpec((1, d), const),
                  pl.BlockSpec((d, LANES), const), pl.BlockSpec((1, LANES), const),
                  pl.BlockSpec((tm, tm), const)],
        out_specs=[pl.BlockSpec((tm, d), row), pl.BlockSpec((tm * ROW_TILES, LANES), row),
                   pl.BlockSpec((tm, LANES), row), pl.BlockSpec((1, LANES), const)],
        out_shape=[jax.ShapeDtypeStruct((t, d), F32), jax.ShapeDtypeStruct((t * ROW_TILES, LANES), F32),
                   jax.ShapeDtypeStruct((t, LANES), F32), jax.ShapeDtypeStruct((1, LANES), F32)],
        compiler_params=_params("arbitrary"),
        name="post_mixer",
    )(x, oa_p, oa_s, ob_p, ob_s, gz, modtab, gon, w_out, g2.reshape(1, d), rw, rb, lstrict)


def _tile_copy(src, src_row, dst, dst_row, sem):
    return pltpu.make_async_copy(src.at[pl.ds(pl.multiple_of(src_row, SUBLANES), SUBLANES)],
                                 dst.at[pl.ds(pl.multiple_of(dst_row, SUBLANES), SUBLANES)], sem)


def _load_indices(idx_vmem_ref, idx_smem, sem):
    load = pltpu.make_async_copy(idx_vmem_ref.at[0, 0], idx_smem, sem)
    load.start()
    load.wait()


def _expert_kernel(be_ref, nb_ref, tok0_ref, tok1_ref, tokn_ref, h2_hbm, wgu_ref, bgu_ref, wd_ref, bd_ref, p_ref,
                   y_ref, wgu_s, wd_s, xbuf, idx_smem, idx_sem, sems):
    i = pl.program_id(0)
    nb = nb_ref[0]
    bm = EXPERT_BLOCK
    slot = lax.rem(i, EXPERT_SLOTS)

    def gather(tok_ref, into):
        _load_indices(tok_ref, idx_smem, idx_sem)
        for r in range(bm):
            _tile_copy(h2_hbm, idx_smem[r], xbuf.at[into], r * SUBLANES, sems.at[into]).start()

    @pl.when(i == 0)
    def _():
        gather(tok0_ref, 0)

    @pl.when(jnp.logical_and(i == 0, 1 < nb))
    def _():
        gather(tok1_ref, 1)

    @pl.when(i + 2 < nb)
    def _():
        gather(tokn_ref, lax.rem(i + 2, EXPERT_SLOTS))

    e = be_ref[i]
    prev = be_ref[jnp.maximum(i - 1, 0)]

    @pl.when(jnp.logical_or(i == 0, e != prev))
    def _():
        p = p_ref[...]
        for j in range(2 * D_FF // (2 * LANES)):
            w = wgu_ref[0, 0, :, j * 2 * LANES:(j + 1) * 2 * LANES].astype(BF16)
            r = jnp.dot(w, p, preferred_element_type=F32).astype(BF16)
            wgu_s[:, j * LANES:(j + 1) * LANES] = r[:, :LANES]
            wgu_s[:, D_FF + j * LANES:D_FF + (j + 1) * LANES] = r[:, LANES:]
        wd_s[...] = wd_ref[0, 0].astype(BF16)

    @pl.when(i < nb)
    def _():
        for r in range(bm):
            _tile_copy(h2_hbm, 0, xbuf.at[slot], 0, sems.at[slot]).wait()
        x = _to_rows(xbuf.at[slot], bm)
        gu = jnp.dot(x.astype(BF16), wgu_s[...], preferred_element_type=F32) + bgu_ref[0]
        gate = jnp.minimum(gu[:, :D_FF], SWIGLU_LIMIT)
        up = jnp.clip(gu[:, D_FF:], -SWIGLU_LIMIT, SWIGLU_LIMIT)
        hdn = (up + 1.0) * (gate * _sigmoid(SWIGLU_ALPHA * gate))
        _store_rows(y_ref, jnp.dot(hdn.astype(BF16), wd_s[...], preferred_element_type=F32) + bd_ref[0])

    @pl.when(i >= nb)
    def _():
        y_ref[...] = jnp.zeros_like(y_ref)


def _experts(layer, blk_e, nblk, tok3, h2t, w_gu, b_gu_r, w_down, b_down):
    n_blocks = tok3.shape[0]
    bm = EXPERT_BLOCK
    d = D_MODEL
    n_e, n_gu = w_gu.shape[1], w_gu.shape[3]
    w4 = lambda i, be, nb: (layer, be[i], 0, 0)
    b3 = lambda i, be, nb: (be[i], 0, 0)
    grid_spec = pltpu.PrefetchScalarGridSpec(
        num_scalar_prefetch=2,
        grid=(n_blocks,),
        in_specs=[pl.BlockSpec((1, 1, bm), lambda i, be, nb: (0, 0, 0)),
                  pl.BlockSpec((1, 1, bm), lambda i, be, nb: (1, 0, 0)),
                  pl.BlockSpec((1, 1, bm), lambda i, be, nb: (jnp.minimum(i + 2, n_blocks - 1), 0, 0)),
                  pl.BlockSpec(memory_space=pl.ANY),
                  pl.BlockSpec((1, 1, d, n_gu), w4), pl.BlockSpec((1, 1, n_gu), b3),
                  pl.BlockSpec((1, 1, D_FF, d), w4), pl.BlockSpec((1, 1, d), b3),
                  pl.BlockSpec((2 * LANES, 2 * LANES), lambda i, be, nb: (0, 0))],
        out_specs=pl.BlockSpec((bm * ROW_TILES, LANES), lambda i, be, nb: (i, 0)),
        scratch_shapes=[pltpu.VMEM((d, n_gu), BF16), pltpu.VMEM((D_FF, d), BF16),
                        pltpu.VMEM((EXPERT_SLOTS, bm * ROW_TILES, LANES), F32), pltpu.SMEM((bm,), jnp.int32),
                        pltpu.SemaphoreType.DMA, pltpu.SemaphoreType.DMA((EXPERT_SLOTS,))],
    )
    return pl.pallas_call(
        _expert_kernel,
        grid_spec=grid_spec,
        out_shape=jax.ShapeDtypeStruct((n_blocks * bm * ROW_TILES, LANES), F32),
        compiler_params=_params("arbitrary"),
        name="moe_experts",
    )(blk_e, nblk, tok3, tok3, tok3, h2t, w_gu, b_gu_r.reshape(n_e, 1, n_gu), w_down, b_down.reshape(n_e, 1, d),
      _deinterleave_matrix())


def _combine_kernel(d0_ref, dn_ref, x1_ref, route_ref, mod_ref, gf_ref, yb_hbm, o_ref, idx_smem, rows, idx_sem,
                    sems, *, final, ntiles):
    i = pl.program_id(0)
    tm = x1_ref.shape[0]
    n = tm * TOP_K
    slot = lax.rem(i, 2)

    def gather(dest_ref, into):
        _load_indices(dest_ref, idx_smem, idx_sem)
        for j in range(n):
            _tile_copy(yb_hbm, idx_smem[j], rows.at[into, j % TOP_K], (j // TOP_K) * SUBLANES,
                       sems.at[into]).start(priority=j % 2)

    @pl.when(i == 0)
    def _():
        gather(d0_ref, 0)

    @pl.when(i + 1 < ntiles)
    def _():
        gather(dn_ref, 1 - slot)

    for j in range(n):
        _tile_copy(yb_hbm, 0, rows.at[slot, 0], 0, sems.at[slot]).wait()
    route = route_ref[...]
    f = _to_rows(rows.at[slot, 0], tm) * route[:, TOP_K:TOP_K + 1]
    for kk in range(1, TOP_K):
        f = f + _to_rows(rows.at[slot, kk], tm) * route[:, TOP_K + kk:TOP_K + kk + 1]
    x2 = x1_ref[...] + mod_ref[0, 5:6, :] * f
    o_ref[...] = _rms(x2) * gf_ref[...] if final else x2


def _combine(dest3, x1, route, modtab, g_final, yb, group_of_tile, final):
    t, d = x1.shape
    tm = TOKEN_TILE
    ntiles = t // tm
    n = dest3.shape[-1]
    row = lambda i: (i, 0)
    return pl.pallas_call(
        functools.partial(_combine_kernel, final=final, ntiles=ntiles),
        grid=(ntiles,),
        in_specs=[pl.BlockSpec((1, 1, n), lambda i: (0, 0, 0)),
                  pl.BlockSpec((1, 1, n), lambda i: (jnp.minimum(i + 1, ntiles - 1), 0, 0)),
                  pl.BlockSpec((tm, d), row), pl.BlockSpec((tm, LANES), row),
                  pl.BlockSpec((1, 8, d), lambda i: (group_of_tile(i), 0, 0)),
                  pl.BlockSpec((1, d), lambda i: (0, 0)),
                  pl.BlockSpec(memory_space=pl.ANY)],
        out_specs=pl.BlockSpec((tm, d), row),
        out_shape=jax.ShapeDtypeStruct((t, d), F32),
        scratch_shapes=[pltpu.SMEM((n,), jnp.int32), pltpu.VMEM((2, TOP_K, tm * ROW_TILES, LANES), F32),
                        pltpu.SemaphoreType.DMA, pltpu.SemaphoreType.DMA((2,))],
        compiler_params=_params("arbitrary"),
        name="moe_combine",
    )(dest3, dest3, x1, route, modtab, g_final.reshape(1, d), yb)


def _grid_swap(x, r1, r2):
    n, d = x.shape
    b = n // (r1 * r2)
    return x.reshape(b, r1, r2, d).swapaxes(1, 2).reshape(n, d)


def _prep_w_in(w_in):
    sizes = (QA_W, QA_W, VA_W, GATE_RANK, VA_W, QKVB_W, 2 * H_B, 2 * H_B, VB_W)
    offs = np.concatenate([[0], np.cumsum(sizes)])
    qa, ka, va, glr, oga, qkvb, ab, bb, zb = [w_in[..., offs[i]:offs[i + 1]] for i in range(len(sizes))]
    pad = jnp.zeros(w_in.shape[:-1] + (SM_W - GATE_RANK - 4 * H_B,), w_in.dtype)
    return jnp.concatenate([qa, ka, va, oga, zb, qkvb, glr, ab, bb, pad], axis=-1).astype(BF16)


def kernel(x_prompt, x_sample, state_gla, state_gdn, c, c_ctx, w_mod, b_mod, g_norm1, g_norm2, w_in,
           gla_w_gate2, gla_b_gate, gla_g_onorm, gdn_conv_w, gdn_a_log, gdn_dt_bias, gdn_g_onorm, w_out,
           router_w, router_b, w_gu, b_gu, w_down, b_down, g_final):
    bp, sp, d = x_prompt.shape
    bs, ns, _ = x_sample.shape
    depth = w_mod.shape[0]
    n_p, n_s = bp * sp, bs * ns
    t = n_p + n_s
    tm = TOKEN_TILE
    nsp, nss = SEQS_PER_STEP_SHORT, SEQS_PER_STEP_LONG
    assert d == D_MODEL and sp % tm == 0 and ns % tm == 0 and bs + 1 <= 8
    assert bp % nsp == 0 and bs % nss == 0 and n_p % (nss * ns) == 0
    rows_s = ns // GRID_W

    def group_of_tile(i):
        return jnp.where(i < n_p // tm, 0, 1 + (i - n_p // tm) // (ns // tm))

    cond = jnp.concatenate([c_ctx[None], c, jnp.zeros((8 - 1 - bs, d), F32)], axis=0)
    mods = _modulation(cond, w_mod, b_mod)
    mods = mods.reshape(depth, 8, N_MOD, d)[:, :1 + bs]
    modtab = jnp.concatenate([mods, jnp.zeros((depth, 1 + bs, 8 - N_MOD, d), F32)], axis=2)

    w_in_r = _prep_w_in(w_in)
    w_out_b = w_out.astype(BF16)
    wg = jnp.concatenate([gla_w_gate2, jnp.zeros((depth, 2, SM_W - GATE_RANK, QA_W), F32)], axis=2)
    wgh = wg.astype(BF16)
    wgl = (wg - wgh.astype(F32)).astype(BF16)
    bg = gla_b_gate.reshape(depth, 2, 1, QA_W)
    lane_pad = lambda v, off: jnp.pad(v.reshape(depth, 1, -1), ((0, 0), (0, 0), (off, SM_W - off - v[0].size)))
    a_log_vec = lane_pad(gdn_a_log, SM_A0)
    dtb_vec = lane_pad(gdn_dt_bias, SM_A0)
    conv_w = jnp.pad(gdn_conv_w, ((0, 0), (0, 8 - gdn_conv_w.shape[1]), (0, 0)))
    gon = jnp.stack([gla_g_onorm, gdn_g_onorm], axis=1)
    rw = jnp.pad(router_w, ((0, 0), (0, 0), (0, LANES - N_EXPERTS)))
    rb = jnp.pad(router_b, ((0, 0), (0, LANES - N_EXPERTS)), constant_values=-1e30).reshape(depth, 1, LANES)
    b_gu_r = jnp.concatenate([b_gu[..., 0::2], b_gu[..., 1::2]], axis=-1)

    hsel = jnp.eye(H_A, dtype=F32)
    st_a_s = jnp.einsum("bldhkv,hg->bldhvgk", state_gla.astype(F32), hsel).reshape(bs, depth, 2, VA_W, QA_W)
    st_b_s = state_gdn.astype(F32)
    st_a_0 = jnp.zeros((nsp, 2, VA_W, QA_W), F32)
    st_b_0 = jnp.zeros((nsp, 2, H_B, DK_B, DV_B), F32)

    n_blocks = (t * TOP_K) // EXPERT_BLOCK + N_EXPERTS
    cap = n_blocks * EXPERT_BLOCK
    tok_of_slot = jnp.repeat(jnp.arange(t, dtype=jnp.int32) * ROW_TILES, TOP_K)

    x = jnp.concatenate([x_prompt.reshape(n_p, d), x_sample.reshape(n_s, d)], axis=0)
    new_gla, new_gdn = [], []
    for l in range(depth):
        if l % 2 == 1:
            x = lax.dynamic_update_slice(x, _grid_swap(x[n_p:], rows_s, GRID_W), (n_p, 0))
        pa, gz, pb, sm = _pre_mixer(x, modtab[l], g_norm1[l], w_in_r[l], group_of_tile)
        oa_p, sa = _gla(pa, sm, st_a_0, wgh[l], wgl[l], bg[l], ntotal=bp, seqlen=sp, nseq=nsp, row_block0=0,
                        shared_state=True)
        oa_s, _ = _gla(pa, sm, st_a_s[:, l], wgh[l], wgl[l], bg[l], ntotal=bs, seqlen=ns, nseq=nss,
                       row_block0=n_p // (nss * ns), shared_state=False)
        ob_p, sb = _gdn(pb, sm, st_b_0, conv_w[l], a_log_vec[l], dtb_vec[l], ntotal=bp, seqlen=sp, nseq=nsp,
                        row_block0=0, shared_state=True)
        ob_s, _ = _gdn(pb, sm, st_b_s[:, l], conv_w[l], a_log_vec[l], dtb_vec[l], ntotal=bs, seqlen=ns, nseq=nss,
                       row_block0=n_p // (nss * ns), shared_state=False)
        new_gla.append(sa)
        new_gdn.append(sb)
        x1, h2t, route, counts = _post_mixer(x, oa_p, oa_s, ob_p, ob_s, gz, modtab[l], gon[l], w_out_b[l],
                                             g_norm2[l], rw[l], rb[l], group_of_tile)

        top_i = route[:, :TOP_K].astype(jnp.int32)
        rank = route[:, 2 * TOP_K:3 * TOP_K].astype(jnp.int32)
        sizes = counts[0, :N_EXPERTS].astype(jnp.int32)
        padded = (sizes + EXPERT_BLOCK - 1) // EXPERT_BLOCK * EXPERT_BLOCK
        pend = jnp.cumsum(padded)
        pstart = pend - padded
        onehot = top_i[..., None] == jnp.arange(N_EXPERTS, dtype=jnp.int32)
        dest = (rank + jnp.sum(jnp.where(onehot, pstart, 0), axis=-1)).reshape(-1)
        dest3 = (dest * ROW_TILES).reshape(t // tm, 1, tm * TOP_K)
        tok3 = jnp.zeros((cap,), jnp.int32).at[dest].set(tok_of_slot, unique_indices=True)
        tok3 = tok3.reshape(n_blocks, 1, EXPERT_BLOCK)
        nblk = (pend[-1] // EXPERT_BLOCK).reshape(1)
        blk_start = jnp.minimum(jnp.arange(n_blocks, dtype=jnp.int32), nblk[0] - 1) * EXPERT_BLOCK
        blk_e = jnp.sum(blk_start[:, None] >= pend[None, :], axis=-1).astype(jnp.int32)

        yb = _experts(l, blk_e, nblk, tok3, h2t, w_gu, b_gu_r[l], w_down, b_down[l])
        x = _combine(dest3, x1, route, modtab[l], g_final, yb, group_of_tile, final=(l == depth - 1))
        if l % 2 == 1:
            x = lax.dynamic_update_slice(x, _grid_swap(x[n_p:], GRID_W, rows_s), (n_p, 0))

    y_prompt = x[:n_p].reshape(bp, sp, d)
    y_sample = x[n_p:].reshape(bs, ns, d)
    sa_all = jnp.stack(new_gla, axis=1).reshape(bp, depth, 2, H_A, DV_A, H_A, DK_A)
    new_state_gla = jnp.einsum("bldhvgk,hg->bldhkv", sa_all, hsel)
    new_state_gdn = jnp.stack(new_gdn, axis=1)
    return (y_prompt, y_sample, new_state_gla, new_state_gdn)
```

```python
import functools

import numpy as np
import jax
import jax.numpy as jnp
from jax import lax
from jax.experimental import pallas as pl
from jax.experimental.pallas import tpu as pltpu

F32 = jnp.float32
BF16 = jnp.bfloat16

D_MODEL = 1024
GRID_W = 64
H_A, DK_A, DV_A = 4, 64, 128
H_B, DK_B, DV_B = 4, 128, 128
GATE_RANK = 16
GATE_NORMALIZER = 16.0
CHUNK = 64
N_EXPERTS = 32
TOP_K = 4
D_FF = 1024
SWIGLU_LIMIT = 7.0
SWIGLU_ALPHA = 1.702
N_MOD = 6
EPS = 1e-6

QA_W = H_A * DK_A
VA_W = H_A * DV_A
QB_W = H_B * DK_B
VB_W = H_B * DV_B
QKVB_W = 2 * QB_W + VB_W
PA_W = 2 * QA_W + VA_W
GZ_W = VA_W + VB_W
SM_W = 128
SM_A0 = GATE_RANK
SM_B0 = GATE_RANK + 2 * H_B
W_IN_COLS = PA_W + GZ_W + QKVB_W + SM_W

TOKEN_TILE = 256
EXPERT_BLOCK = 256
EXPERT_SLOTS = 3
COMBINE_SLOTS = 3
LANES = 128
SUBLANES = 8
ROW_TILES = D_MODEL // LANES
BF16_ROWS = 16
VMEM_LIMIT_BYTES = 56 * 2**20
SEQS_PER_STEP_SHORT = 4
SEQS_PER_STEP_LONG = 2

_NT = (((1,), (1,)), ((), ()))
_TN = (((0,), (0,)), ((), ()))

assert ROW_TILES == SUBLANES


def _params(*sem):
    return pltpu.CompilerParams(dimension_semantics=sem, vmem_limit_bytes=VMEM_LIMIT_BYTES)


def _dot(a, b):
    return jnp.dot(a.astype(BF16), b.astype(BF16), preferred_element_type=F32)


def _dot_nt(a, b):
    return lax.dot_general(a.astype(BF16), b.astype(BF16), _NT, preferred_element_type=F32)


def _dot_tn(a, b):
    return lax.dot_general(a.astype(BF16), b.astype(BF16), _TN, preferred_element_type=F32)


def _split(x):
    hi = x.astype(BF16)
    lo = (x - hi.astype(F32)).astype(BF16)
    return hi, lo


def _dot_x2(m, x):
    hi, lo = _split(x)
    return _dot(m, hi) + _dot(m, lo)


def _dot_2x(x, m):
    hi, lo = _split(x)
    return _dot(hi, m) + _dot(lo, m)


def _dot3(a, b):
    ah, al = _split(a)
    bh, bl = _split(b)
    return _dot(ah, bh) + _dot(ah, bl) + _dot(al, bh)


def _sigmoid(x):
    return 1.0 / (1.0 + jnp.exp(-x))


def _softplus(x):
    return jnp.maximum(x, 0.0) + jnp.log(1.0 + jnp.exp(-jnp.abs(x)))


def _rms(x):
    return x * lax.rsqrt(jnp.mean(x * x, axis=-1, keepdims=True) + EPS)


def _stack4(x, mask):
    xb = x.astype(BF16)
    return jnp.concatenate([xb, xb, xb, xb], axis=0) * mask


def _to_rows(tiles_ref, n):
    return jnp.concatenate([tiles_ref[pl.ds(j, n, stride=ROW_TILES), :] for j in range(ROW_TILES)], axis=1)


def _store_rows(tiles_ref, x):
    n = x.shape[0]
    for j in range(ROW_TILES):
        tiles_ref[pl.ds(j, n, stride=ROW_TILES), :] = x[:, j * LANES:(j + 1) * LANES]


def _block_mask(rows, row_blk, cols, col_blk):
    r = np.arange(rows)[:, None] // row_blk
    c = np.arange(cols)[None, :] // col_blk
    return (r == c).astype(np.float32)


@functools.lru_cache(maxsize=None)
def _gla_consts():
    c = CHUNK
    w = np.zeros((7, c, c), np.float32)
    m = np.zeros((7, c, c), np.float32)
    w[0] = np.tril(np.ones((c, c)))
    m[0] = np.eye(c)
    i = np.arange(c)
    for l, s in enumerate((32, 16, 8, 4, 2, 1), start=1):
        ref = (i // (2 * s)) * 2 * s + s - 1
        for r in range(c):
            if r > ref[r]:
                w[l, r, ref[r] + 1:r + 1] = 1.0
            else:
                w[l, r, r + 1:ref[r] + 1] = 1.0
        same = (i[:, None] // (2 * s)) == (i[None, :] // (2 * s))
        m[l] = same & ((i[:, None] % (2 * s)) >= s) & ((i[None, :] % (2 * s)) < s)
    assert np.array_equal(m.sum(0), np.tril(np.ones((c, c))))
    wall = np.stack([w.reshape(7 * c, c), w[:, ::-1, ::-1].reshape(7 * c, c)])
    masks = np.stack([np.tile(m, (1, 1, H_A)), np.tile(m[:, ::-1, ::-1], (1, 1, H_A))])
    return dict(
        wall=jnp.asarray(wall, BF16),
        masks=jnp.asarray(masks, F32),
        hm=jnp.asarray(_block_mask(H_A * c, c, QA_W, DK_A), BF16),
        vm=jnp.asarray(_block_mask(H_A * c, c, VA_W, DV_A), BF16),
        bd=jnp.asarray(_block_mask(VA_W, DV_A, QA_W, DK_A), F32),
    )


@functools.lru_cache(maxsize=None)
def _gdn_consts():
    c = CHUNK
    tril = np.tril(np.ones((c, c), np.float32))
    lc = np.stack([tril, tril.T])
    ut = np.stack([np.tile(lc[d].T, (1, H_B)) for d in range(2)])
    cm = np.stack([np.tile(lc[d], (1, H_B)) for d in range(2)])
    eye = np.tile(np.eye(c, dtype=np.float32), (1, H_B))
    sm = cm - eye[None]
    i = np.arange(c)
    dblk = np.tile(((i[:, None] // 16) == (i[None, :] // 16)).astype(np.float32), (1, H_B))
    eg = np.zeros((2, SM_W, H_B * c), np.float32)
    eb = np.zeros((2, SM_W, H_B * c), np.float32)
    egw = np.zeros((2, SM_W, QB_W), np.float32)
    ebw = np.zeros((2, SM_W, QB_W), np.float32)
    for d in range(2):
        for h in range(H_B):
            eg[d, SM_A0 + d * H_B + h, h * c:(h + 1) * c] = 1.0
            eb[d, SM_B0 + d * H_B + h, h * c:(h + 1) * c] = 1.0
            egw[d, SM_A0 + d * H_B + h, h * DK_B:(h + 1) * DK_B] = 1.0
            ebw[d, SM_B0 + d * H_B + h, h * DK_B:(h + 1) * DK_B] = 1.0
    return dict(
        lc=jnp.asarray(lc, BF16), ut=jnp.asarray(ut, F32), cm=jnp.asarray(cm, F32), sm=jnp.asarray(sm, F32),
        eye=jnp.asarray(eye, F32), dblk=jnp.asarray(dblk, F32),
        eg=jnp.asarray(eg, BF16), eb=jnp.asarray(eb, BF16), egw=jnp.asarray(egw, BF16), ebw=jnp.asarray(ebw, BF16),
        hm=jnp.asarray(_block_mask(H_B * c, c, H_B * c, c), BF16),
        km=jnp.asarray(_block_mask(H_B * c, c, QB_W, DK_B), BF16),
    )


@functools.lru_cache(maxsize=None)
def _deinterleave_matrix():
    p = np.zeros((2 * LANES, 2 * LANES), np.float32)
    j = np.arange(LANES)
    p[2 * j, j] = 1.0
    p[2 * j + 1, LANES + j] = 1.0
    return jnp.asarray(p, BF16)


def _mod_kernel(c_ref, w_ref, b_ref, o_ref):
    c = c_ref[...]
    o_ref[0] = _dot3(c * _sigmoid(c), w_ref[0]) + b_ref[0]


def _modulation(cond, w_mod, b_mod):
    depth, d, n = w_mod.shape
    tn = 1536
    return pl.pallas_call(
        _mod_kernel,
        grid=(depth, n // tn),
        in_specs=[pl.BlockSpec((8, d), lambda l, j: (0, 0)),
                  pl.BlockSpec((1, d, tn), lambda l, j: (l, 0, j)),
                  pl.BlockSpec((1, 1, tn), lambda l, j: (l, 0, j))],
        out_specs=pl.BlockSpec((1, 8, tn), lambda l, j: (l, 0, j)),
        out_shape=jax.ShapeDtypeStruct((depth, 8, n), F32),
        compiler_params=_params("parallel", "parallel"),
        name="modulation",
    )(cond, w_mod, b_mod.reshape(depth, 1, n))


def _pre_kernel(x_ref, mod_ref, g_ref, w_ref, pa_ref, gz_ref, pb_ref, sm_ref):
    h = _rms(x_ref[...]) * g_ref[...]
    h = h * (1.0 + mod_ref[0, 1:2, :]) + mod_ref[0, 0:1, :]
    p = jnp.dot(h.astype(BF16), w_ref[...], preferred_element_type=F32)
    pa_ref[...] = p[:, :PA_W].astype(BF16)
    gz_ref[...] = p[:, PA_W:PA_W + GZ_W].astype(BF16)
    pb_ref[...] = p[:, PA_W + GZ_W:PA_W + GZ_W + QKVB_W].astype(BF16)
    sm_ref[...] = p[:, PA_W + GZ_W + QKVB_W:]


def _pre_mixer(x, modtab, g1, w_in_r, group_of_tile):
    t, d = x.shape
    tm = TOKEN_TILE
    row = lambda i: (i, 0)
    return pl.pallas_call(
        _pre_kernel,
        grid=(t // tm,),
        in_specs=[pl.BlockSpec((tm, d), row),
                  pl.BlockSpec((1, 8, d), lambda i: (group_of_tile(i), 0, 0)),
                  pl.BlockSpec((1, d), lambda i: (0, 0)),
                  pl.BlockSpec((d, W_IN_COLS), lambda i: (0, 0))],
        out_specs=[pl.BlockSpec((tm, PA_W), row), pl.BlockSpec((tm, GZ_W), row),
                   pl.BlockSpec((tm, QKVB_W), row), pl.BlockSpec((tm, SM_W), row)],
        out_shape=[jax.ShapeDtypeStruct((t, PA_W), BF16), jax.ShapeDtypeStruct((t, GZ_W), BF16),
                   jax.ShapeDtypeStruct((t, QKVB_W), BF16), jax.ShapeDtypeStruct((t, SM_W), F32)],
        compiler_params=_params("parallel"),
        name="pre_mixer",
    )(x, modtab, g1.reshape(1, d), w_in_r)


def _gla_kernel(pa_ref, sm_ref, wall_ref, mask_ref, wgh_ref, wgl_ref, bg_ref, hm_ref, vm_ref, bd_ref, st0_ref,
                o_ref, stf_ref, st_scr, *, nchunk, nseq):
    c = CHUNK
    seqlen = nchunk * c
    hm = hm_ref[...]
    vm = vm_ref[...]
    bd = bd_ref[...]
    st_scr[...] = st0_ref[...]

    chains = [(s, d) for s in range(nseq) for d in range(2)]
    dirs = [d for _, d in chains]

    def body(ci, carry):
        rows = [pl.ds(pl.multiple_of(s * seqlen + (ci if d == 0 else nchunk - 1 - ci) * c, c), c)
                for s, d in chains]
        q = [pa_ref[r, 0:QA_W].astype(F32) * (DK_A ** -0.5) for r in rows]
        k = [pa_ref[r, QA_W:2 * QA_W].astype(F32) for r in rows]
        v = [pa_ref[r, 2 * QA_W:PA_W] for r in rows]
        sm = [_split(sm_ref[r, :]) for r in rows]
        x = _each(lambda y, d: _dot(y[0], wgh_ref[d]) + _dot(y[0], wgl_ref[d]) + _dot(y[1], wgh_ref[d]) + bg_ref[d],
                  sm, dirs)
        gk = [(jnp.minimum(y, 0.0) - jnp.log(1.0 + jnp.exp(-jnp.abs(y)))) * (1.0 / GATE_NORMALIZER) for y in x]
        e_all = _each(lambda g, d: _dot_x2(wall_ref[d], g), gk, dirs)
        a = None
        for l in range(7):
            if l == 0:
                qt, kt = q, k
            else:
                f = [jnp.exp(e[l * c:(l + 1) * c]) for e in e_all]
                qt = _each(lambda y, z: y * z, q, f)
                kt = _each(lambda y, z: y * z, k, f)
            r = _each(lambda y, z, d: _dot_nt(y, _stack4(z, hm)) * mask_ref[d, l], qt, kt, dirs)
            a = r if a is None else _each(lambda y, z: y + z, a, r)
        b = [e[0:c] for e in e_all]
        o = _each(lambda aa, vv, qq, bb, sd: _dot(aa, _stack4(vv, vm)) + _dot_nt(qq * jnp.exp(bb), st_scr[sd]),
                  a, v, q, b, chains)
        for r, d, y in zip(rows, dirs, o):
            o_ref[r, d * VA_W:(d + 1) * VA_W] = y
        b_last = _each(lambda bb, d: bb[c - 1:c] if d == 0 else bb[0:1], b, dirs)
        upd = _each(lambda vv, kk, bl, bb: _dot_tn(vv, kk * jnp.exp(bl - bb)), v, k, b_last, b)
        for sd, bl, up in zip(chains, b_last, upd):
            st_scr[sd] = st_scr[sd] * jnp.exp(bl) + up * bd
        return carry

    lax.fori_loop(0, nchunk, body, 0)
    stf_ref[...] = st_scr[...]


def _gla(pa, sm, st0, wgh, wgl, bg, *, ntotal, seqlen, nseq, row_block0, shared_state):
    cst = _gla_consts()
    nchunk = seqlen // CHUNK
    rows = nseq * seqlen
    seq = lambda b: (row_block0 + b, 0)
    full = lambda *shape: pl.BlockSpec(shape, lambda b: (0,) * len(shape))
    st_in = (lambda b: (0, 0, 0, 0)) if shared_state else (lambda b: (b, 0, 0, 0))
    return pl.pallas_call(
        functools.partial(_gla_kernel, nchunk=nchunk, nseq=nseq),
        grid=(ntotal // nseq,),
        in_specs=[pl.BlockSpec((rows, PA_W), seq), pl.BlockSpec((rows, SM_W), seq),
                  full(2, 7 * CHUNK, CHUNK), full(2, 7, CHUNK, QA_W),
                  full(2, SM_W, QA_W), full(2, SM_W, QA_W), full(2, 1, QA_W),
                  full(QA_W, QA_W), full(QA_W, VA_W), full(VA_W, QA_W),
                  pl.BlockSpec((nseq, 2, VA_W, QA_W), st_in)],
        out_specs=[pl.BlockSpec((rows, 2 * VA_W), lambda b: (b, 0)),
                   pl.BlockSpec((nseq, 2, VA_W, QA_W), lambda b: (b, 0, 0, 0))],
        out_shape=[jax.ShapeDtypeStruct((ntotal * seqlen, 2 * VA_W), F32),
                   jax.ShapeDtypeStruct((ntotal, 2, VA_W, QA_W), F32)],
        scratch_shapes=[pltpu.VMEM((nseq, 2, VA_W, QA_W), F32)],
        compiler_params=_params("parallel"),
        name="gla_l%d" % seqlen,
    )(pa, sm, cst["wall"], cst["masks"], wgh, wgl, bg, cst["hm"], cst["vm"], cst["bd"], st0)


def _each(f, *lists):
    return [f(*xs) for xs in zip(*lists)]


def _mm_heads(xs, ys, hm):
    return _each(lambda x, y: _dot(x, _stack4(y, hm)), xs, ys)


def _unit_tri_inverse(ms, eye, dblk, hm):
    mm = functools.partial(_mm_heads, hm=hm)
    dg = [m * dblk for m in ms]
    lo = _each(lambda m, g: m - g, ms, dg)
    d2 = mm(dg, dg)
    d3 = mm(dg, d2)
    d4 = mm(d2, d2)
    p1 = _each(lambda g, a, b: eye - g + a - b, dg, d2, d3)
    p2 = _each(lambda a, b: a + b, p1, mm(p1, d4))
    d8 = mm(d4, d4)
    td = _each(lambda a, b: a + b, p2, mm(p2, d8))
    e1 = mm(td, lo)
    e2 = mm(e1, e1)
    e3 = mm(e1, e2)
    return mm(_each(lambda a, b, c: eye - a + b - c, e1, e2, e3), td)


def _gdn_kernel(pb_ref, sm_ref, cw_ref, al_ref, dtb_ref, lc_ref, ut_ref, cm_ref, smk_ref, eye_ref, dblk_ref,
                eg_ref, eb_ref, egw_ref, ebw_ref, hm_ref, km_ref, st0_ref, o_ref, stf_ref, qkv_scr, s_scr,
                *, nchunk, nseq):
    c = CHUNK
    seqlen = nchunk * c
    w0 = cw_ref[0:1, :]
    w1 = cw_ref[1:2, :]
    w2 = cw_ref[2:3, :]

    def conv_body(cc, carry):
        for s in range(nseq):
            r0 = pl.multiple_of(s * seqlen + cc * c, c)
            x = pb_ref[pl.ds(r0, c), :].astype(F32)
            p0 = pl.multiple_of(jnp.maximum(r0 - BF16_ROWS, 0), BF16_ROWS)
            n0 = pl.multiple_of(jnp.minimum(r0 + c, nseq * seqlen - BF16_ROWS), BF16_ROWS)
            pg = pb_ref[pl.ds(p0, BF16_ROWS), :]
            ng = pb_ref[pl.ds(n0, BF16_ROWS), :]
            prow = jnp.where(cc > 0, pg.astype(F32)[BF16_ROWS - 1:BF16_ROWS], 0.0)
            nrow = jnp.where(cc < nchunk - 1, ng.astype(F32)[0:1], 0.0)
            rid = lax.broadcasted_iota(jnp.int32, x.shape, 0)
            xp = jnp.where(rid == 0, prow, pltpu.roll(x, 1, axis=0))
            xn = jnp.where(rid == c - 1, nrow, pltpu.roll(x, c - 1, axis=0))
            y = w0 * xp + w1 * x + w2 * xn
            sl = y * _sigmoid(y)
            parts = []
            for h in range(2 * H_B):
                p = sl[:, h * DK_B:(h + 1) * DK_B]
                nrm = lax.rsqrt(jnp.sum(p * p, axis=-1, keepdims=True) + EPS)
                parts.append(p * (nrm * (DK_B ** -0.5 if h < H_B else 1.0)))
            parts.append(sl[:, 2 * QB_W:])
            qkv_scr[pl.ds(r0, c), :] = jnp.concatenate(parts, axis=1).astype(BF16)
        return carry

    lax.fori_loop(0, nchunk, conv_body, 0)

    hm = hm_ref[...]
    km = km_ref[...]
    eye = eye_ref[...]
    dblk = dblk_ref[...]
    ones = jnp.ones((c, c), BF16)
    neg_a = -jnp.exp(al_ref[...])
    dtb = dtb_ref[...]
    s_scr[...] = st0_ref[...]

    chains = [(s, d) for s in range(nseq) for d in range(2)]
    dirs = [d for _, d in chains]
    heads = [slice(h * DK_B, (h + 1) * DK_B) for h in range(H_B)]

    def body(ci, carry):
        rows = [pl.ds(pl.multiple_of(s * seqlen + (ci if d == 0 else nchunk - 1 - ci) * c, c), c)
                for s, d in chains]
        q = [qkv_scr[r, 0:QB_W].astype(F32) for r in rows]
        k = [qkv_scr[r, QB_W:2 * QB_W].astype(F32) for r in rows]
        v = [qkv_scr[r, 2 * QB_W:QKVB_W].astype(F32) for r in rows]
        sm = [sm_ref[r, :] for r in rows]
        g_all = [_split(neg_a * _softplus(x + dtb)) for x in sm]
        beta_all = [_split(_sigmoid(x)) for x in sm]
        two = lambda ref: (lambda x, d: _dot(x[0], ref[d]) + _dot(x[1], ref[d]))
        g_b = _each(two(eg_ref), g_all, dirs)
        g_w = _each(two(egw_ref), g_all, dirs)
        beta_i = _each(two(eb_ref), beta_all, dirs)
        beta_w = _each(two(ebw_ref), beta_all, dirs)
        gam_i = _each(lambda x, d: _dot_x2(lc_ref[d], x), g_b, dirs)
        gam_j = _each(lambda x, d: _dot_x2(ones, x * ut_ref[d]), g_b, dirs)
        gam_w = _each(lambda x, d: _dot_x2(lc_ref[d], x), g_w, dirs)
        dec = _each(lambda a, b, d: jnp.exp(jnp.minimum(a - b, 0.0)) * cm_ref[d], gam_i, gam_j, dirs)
        kkqk = _each(lambda kk, qq: _dot_nt(jnp.concatenate([kk, qq], axis=0), _stack4(kk, km)), k, q)
        m = _each(lambda bi, x, dc, d: smk_ref[d] * bi * x[:c] * dc, beta_i, kkqk, dec, dirs)
        t_inv = _unit_tri_inverse(m, eye, dblk, hm)
        eg_w = [jnp.exp(x) for x in gam_w]
        kh = _each(lambda kk, bw, e: kk * (bw * e), k, beta_w, eg_w)
        qh = _each(lambda qq, e: qq * e, q, eg_w)
        res = [[_dot(jnp.concatenate([a[:, hs], b[:, hs]], axis=0), s_scr[s, d, h])
                for h, hs in enumerate(heads)] for a, b, (s, d) in zip(kh, qh, chains)]
        rhs = _each(lambda vv, bw, r: vv * bw - jnp.concatenate([x[:c] for x in r], axis=1), v, beta_w, res)
        u = _each(lambda t, r: _dot(t, _stack4(r, km)), t_inv, rhs)
        o = _each(lambda r, x, dc, uu: jnp.concatenate([y[c:] for y in r], axis=1)
                  + _dot(x[c:] * dc, _stack4(uu, km)), res, kkqk, dec, u)
        for r, d, x in zip(rows, dirs, o):
            o_ref[r, d * VB_W:(d + 1) * VB_W] = x
        g_last = _each(lambda x, d: x[c - 1:c] if d == 0 else x[0:1], gam_w, dirs)
        kd = _each(lambda kk, gl, gw: kk * jnp.exp(gl - gw), k, g_last, gam_w)
        upd = [[_dot_tn(a[:, hs], b[:, hs]) for hs in heads] for a, b in zip(kd, u)]
        for (s, d), gl, up in zip(chains, g_last, upd):
            for h, hs in enumerate(heads):
                s_scr[s, d, h] = s_scr[s, d, h] * jnp.exp(gl[:, hs]) + up[h]
        return carry

    lax.fori_loop(0, nchunk, body, 0)
    stf_ref[...] = s_scr[...]


def _gdn(pb, sm, st0, conv_w, a_log_vec, dtb_vec, *, ntotal, seqlen, nseq, row_block0, shared_state):
    cst = _gdn_consts()
    nchunk = seqlen // CHUNK
    rows = nseq * seqlen
    hc = H_B * CHUNK
    seq = lambda b: (row_block0 + b, 0)
    full = lambda *shape: pl.BlockSpec(shape, lambda b: (0,) * len(shape))
    st_in = (lambda b: (0, 0, 0, 0, 0)) if shared_state else (lambda b: (b, 0, 0, 0, 0))
    return pl.pallas_call(
        functools.partial(_gdn_kernel, nchunk=nchunk, nseq=nseq),
        grid=(ntotal // nseq,),
        in_specs=[pl.BlockSpec((rows, QKVB_W), seq), pl.BlockSpec((rows, SM_W), seq),
                  full(8, QKVB_W), full(1, SM_W), full(1, SM_W),
                  full(2, CHUNK, CHUNK), full(2, CHUNK, hc), full(2, CHUNK, hc), full(2, CHUNK, hc),
                  full(CHUNK, hc), full(CHUNK, hc),
                  full(2, SM_W, hc), full(2, SM_W, hc), full(2, SM_W, QB_W), full(2, SM_W, QB_W),
                  full(hc, hc), full(hc, QB_W),
                  pl.BlockSpec((nseq, 2, H_B, DK_B, DV_B), st_in)],
        out_specs=[pl.BlockSpec((rows, 2 * VB_W), lambda b: (b, 0)),
                   pl.BlockSpec((nseq, 2, H_B, DK_B, DV_B), lambda b: (b, 0, 0, 0, 0))],
        out_shape=[jax.ShapeDtypeStruct((ntotal * seqlen, 2 * VB_W), F32),
                   jax.ShapeDtypeStruct((ntotal, 2, H_B, DK_B, DV_B), F32)],
        scratch_shapes=[pltpu.VMEM((rows, QKVB_W), BF16), pltpu.VMEM((nseq, 2, H_B, DK_B, DV_B), F32)],
        compiler_params=_params("parallel"),
        name="gdn_l%d" % seqlen,
    )(pb, sm, conv_w, a_log_vec, dtb_vec, cst["lc"], cst["ut"], cst["cm"], cst["sm"], cst["eye"], cst["dblk"],
      cst["eg"], cst["eb"], cst["egw"], cst["ebw"], cst["hm"], cst["km"], st0)


def _post_kernel(x_ref, oap_ref, oas_ref, obp_ref, obs_ref, gz_ref, mod_ref, gon_ref, wo_ref, g2_ref, rw_ref,
                 rb_ref, ls_ref, x1_ref, h2_ref, route_ref, cnt_ref, *, context_tiles):
    is_context = pl.program_id(0) < context_tiles

    @pl.when(pl.program_id(0) == 0)
    def _():
        cnt_ref[...] = jnp.zeros_like(cnt_ref)

    gz = gz_ref[...]
    heads = []
    for grp, (op_ref, os_ref) in enumerate(((oap_ref, oas_ref), (obp_ref, obs_ref))):
        o2 = jnp.where(is_context, op_ref[...], os_ref[...])
        o = o2[:, :VA_W] + o2[:, VA_W:]
        for h in range(H_A):
            hs = slice(h * DV_A, (h + 1) * DV_A)
            gate = gz[:, grp * VA_W + h * DV_A:grp * VA_W + (h + 1) * DV_A].astype(F32)
            heads.append(_rms(o[:, hs]) * gon_ref[grp:grp + 1, :] * (gate * _sigmoid(gate)))
    mixed = jnp.concatenate(heads, axis=1)
    y = jnp.dot(mixed.astype(BF16), wo_ref[...], preferred_element_type=F32)
    x1 = x_ref[...] + mod_ref[0, 2:3, :] * y
    x1_ref[...] = x1
    h2 = _rms(x1) * g2_ref[...]
    h2 = h2 * (1.0 + mod_ref[0, 4:5, :]) + mod_ref[0, 3:4, :]
    _store_rows(h2_ref, h2)

    logits = _dot3(h2, rw_ref[...]) + rb_ref[...]
    lane = lax.broadcasted_iota(jnp.int32, logits.shape, 1).astype(F32)
    vals, idxs = [], []
    rem = logits
    for _ in range(TOP_K):
        mx = jnp.max(rem, axis=-1, keepdims=True)
        ix = jnp.min(jnp.where(rem == mx, lane, float(LANES)), axis=-1, keepdims=True)
        vals.append(mx)
        idxs.append(ix)
        rem = jnp.where(lane == ix, -jnp.inf, rem)
    exps = [jnp.exp(v - vals[0]) for v in vals]
    inv = 1.0 / (exps[0] + exps[1] + exps[2] + exps[3])
    onehot = jnp.zeros(logits.shape, F32)
    for ix in idxs:
        onehot = onehot + (lane == ix).astype(F32)
    before = _dot(ls_ref[...], onehot) + cnt_ref[...]
    route = jnp.zeros(logits.shape, F32)
    for kk in range(TOP_K):
        rank = jnp.sum(jnp.where(lane == idxs[kk], before, 0.0), axis=-1, keepdims=True)
        route = jnp.where(lane == kk, idxs[kk], route)
        route = jnp.where(lane == TOP_K + kk, exps[kk] * inv, route)
        route = jnp.where(lane == 2 * TOP_K + kk, rank, route)
    route_ref[...] = route
    cnt_ref[...] = cnt_ref[...] + jnp.sum(onehot, axis=0, keepdims=True)


def _post_mixer(x, oa_p, oa_s, ob_p, ob_s, gz, modtab, gon, w_out, g2, rw, rb, group_of_tile):
    t, d = x.shape
    tm = TOKEN_TILE
    context_tiles = oa_p.shape[0] // tm
    row = lambda i: (i, 0)
    ctx_row = lambda i: (jnp.minimum(i, context_tiles - 1), 0)
    lat_row = lambda i: (jnp.maximum(i - context_tiles, 0), 0)
    const = lambda i: (0, 0)
    lstrict = jnp.asarray(np.tril(np.ones((tm, tm), np.float32), -1), BF16)
    return pl.pallas_call(
        functools.partial(_post_kernel, context_tiles=context_tiles),
        grid=(t // tm,),
        in_specs=[pl.BlockSpec((tm, d), row),
                  pl.BlockSpec((tm, 2 * VA_W), ctx_row), pl.BlockSpec((tm, 2 * VA_W), lat_row),
                  pl.BlockSpec((tm, 2 * VB_W), ctx_row), pl.BlockSpec((tm, 2 * VB_W), lat_row),
                  pl.BlockSpec((tm, GZ_W), row),
                  pl.BlockSpec((1, 8, d), lambda i: (group_of_tile(i), 0, 0)),
                  pl.BlockSpec((2, DV_A), const), pl.BlockSpec((d, d), const), pl.BlockSpec((1, d), const),
                  pl.BlockSpec((d, LANES), const), pl.BlockSpec((1, LANES), const),
                  pl.BlockSpec((tm, tm), const)],
        out_specs=[pl.BlockSpec((tm, d), row), pl.BlockSpec((tm * ROW_TILES, LANES), row),
                   pl.BlockSpec((tm, LANES), row), pl.BlockSpec((1, LANES), const)],
        out_shape=[jax.ShapeDtypeStruct((t, d), F32), jax.ShapeDtypeStruct((t * ROW_TILES, LANES), F32),
                   jax.ShapeDtypeStruct((t, LANES), F32), jax.ShapeDtypeStruct((1, LANES), F32)],
        compiler_params=_params("arbitrary"),
        name="post_mixer",
    )(x, oa_p, oa_s, ob_p, ob_s, gz, modtab, gon, w_out, g2.reshape(1, d), rw, rb, lstrict)


def _tile_copy(src, src_row, dst, dst_row, sem):
    return pltpu.make_async_copy(src.at[pl.ds(pl.multiple_of(src_row, SUBLANES), SUBLANES)],
                                 dst.at[pl.ds(pl.multiple_of(dst_row, SUBLANES), SUBLANES)], sem)


def _expert_kernel(be_ref, nb_ref, tok0_ref, tok1_ref, tokn_ref, h2_hbm, wgu_ref, bgu_ref, wd_ref, bd_ref,
                   p_ref, y_ref, wgu_s, wd_s, xbuf, idx_smem, idx_sem, sems):
    i = pl.program_id(0)
    nb = nb_ref[0]
    bm = EXPERT_BLOCK
    slot = lax.rem(i, EXPERT_SLOTS)

    def index_copy(tok_ref):
        return pltpu.make_async_copy(tok_ref.at[0, 0], idx_smem, idx_sem)

    def issue(into):
        for r in range(bm):
            _tile_copy(h2_hbm, idx_smem[r], xbuf.at[into], r * SUBLANES, sems.at[into]).start()

    @pl.when(i == 0)
    def _():
        first = index_copy(tok0_ref)
        first.start()
        first.wait()
        issue(0)

    @pl.when(jnp.logical_and(i == 0, 1 < nb))
    def _():
        second = index_copy(tok1_ref)
        second.start()
        second.wait()
        issue(1)

    @pl.when(i + 2 < nb)
    def _():
        index_copy(tokn_ref).start()

    e = be_ref[i]
    prev = be_ref[jnp.maximum(i - 1, 0)]

    @pl.when(jnp.logical_or(i == 0, e != prev))
    def _():
        p = p_ref[...]
        for j in range(2 * D_FF // (2 * LANES)):
            w = wgu_ref[0, 0, :, j * 2 * LANES:(j + 1) * 2 * LANES].astype(BF16)
            r = jnp.dot(w, p, preferred_element_type=F32).astype(BF16)
            wgu_s[:, j * LANES:(j + 1) * LANES] = r[:, :LANES]
            wgu_s[:, D_FF + j * LANES:D_FF + (j + 1) * LANES] = r[:, LANES:]
        wd_s[...] = wd_ref[0, 0].astype(BF16)

    @pl.when(i < nb)
    def _():
        for r in range(bm):
            _tile_copy(h2_hbm, 0, xbuf.at[slot], 0, sems.at[slot]).wait()
        x = _to_rows(xbuf.at[slot], bm)
        gu = jnp.dot(x.astype(BF16), wgu_s[...], preferred_element_type=F32) + bgu_ref[0]
        gate = jnp.minimum(gu[:, :D_FF], SWIGLU_LIMIT)
        up = jnp.clip(gu[:, D_FF:], -SWIGLU_LIMIT, SWIGLU_LIMIT)
        hdn = (up + 1.0) * (gate * _sigmoid(SWIGLU_ALPHA * gate))
        _store_rows(y_ref, jnp.dot(hdn.astype(BF16), wd_s[...], preferred_element_type=F32) + bd_ref[0])

    @pl.when(i >= nb)
    def _():
        y_ref[...] = jnp.zeros_like(y_ref)

    @pl.when(i + 2 < nb)
    def _():
        index_copy(tokn_ref).wait()
        issue(lax.rem(i + 2, EXPERT_SLOTS))


def _experts(layer, blk_e, nblk, tok3, h2t, w_gu, b_gu_r, w_down, b_down):
    n_blocks = tok3.shape[0]
    bm = EXPERT_BLOCK
    d = D_MODEL
    n_e, n_gu = w_gu.shape[1], w_gu.shape[3]
    w4 = lambda i, be, nb: (layer, be[i], 0, 0)
    b3 = lambda i, be, nb: (be[i], 0, 0)
    grid_spec = pltpu.PrefetchScalarGridSpec(
        num_scalar_prefetch=2,
        grid=(n_blocks,),
        in_specs=[pl.BlockSpec((1, 1, bm), lambda i, be, nb: (0, 0, 0)),
                  pl.BlockSpec((1, 1, bm), lambda i, be, nb: (1, 0, 0)),
                  pl.BlockSpec((1, 1, bm), lambda i, be, nb: (jnp.minimum(i + 2, n_blocks - 1), 0, 0)),
                  pl.BlockSpec(memory_space=pl.ANY),
                  pl.BlockSpec((1, 1, d, n_gu), w4), pl.BlockSpec((1, 1, n_gu), b3),
                  pl.BlockSpec((1, 1, D_FF, d), w4), pl.BlockSpec((1, 1, d), b3),
                  pl.BlockSpec((2 * LANES, 2 * LANES), lambda i, be, nb: (0, 0))],
        out_specs=pl.BlockSpec((bm * ROW_TILES, LANES), lambda i, be, nb: (i, 0)),
        scratch_shapes=[pltpu.VMEM((d, n_gu), BF16), pltpu.VMEM((D_FF, d), BF16),
                        pltpu.VMEM((EXPERT_SLOTS, bm * ROW_TILES, LANES), F32), pltpu.SMEM((bm,), jnp.int32),
                        pltpu.SemaphoreType.DMA, pltpu.SemaphoreType.DMA((EXPERT_SLOTS,))],
    )
    return pl.pallas_call(
        _expert_kernel,
        grid_spec=grid_spec,
        out_shape=jax.ShapeDtypeStruct((n_blocks * bm * ROW_TILES, LANES), F32),
        compiler_params=_params("arbitrary"),
        name="moe_experts",
    )(blk_e, nblk, tok3, tok3, tok3, h2t, w_gu, b_gu_r.reshape(n_e, 1, n_gu), w_down, b_down.reshape(n_e, 1, d),
      _deinterleave_matrix())


def _combine_kernel(d0_ref, d1_ref, dn_ref, x1_ref, route_ref, mod_ref, gf_ref, yb_hbm, o_ref, idx_smem, rows,
                    idx_sem, sems, *, final, ntiles):
    i = pl.program_id(0)
    tm = x1_ref.shape[0]
    n = tm * TOP_K
    slot = lax.rem(i, COMBINE_SLOTS)

    def index_copy(dest_ref):
        return pltpu.make_async_copy(dest_ref.at[0, 0], idx_smem, idx_sem)

    def issue(into):
        for j in range(n):
            _tile_copy(yb_hbm, idx_smem[j], rows.at[into, j % TOP_K], (j // TOP_K) * SUBLANES,
                       sems.at[into]).start(priority=j % 2)

    @pl.when(i == 0)
    def _():
        for tile, dest_ref in enumerate((d0_ref, d1_ref)[:min(2, ntiles)]):
            load = index_copy(dest_ref)
            load.start()
            load.wait()
            issue(tile)

    @pl.when(i + 2 < ntiles)
    def _():
        index_copy(dn_ref).start()

    for j in range(n):
        _tile_copy(yb_hbm, 0, rows.at[slot, 0], 0, sems.at[slot]).wait()
    route = route_ref[...]
    f = _to_rows(rows.at[slot, 0], tm) * route[:, TOP_K:TOP_K + 1]
    for kk in range(1, TOP_K):
        f = f + _to_rows(rows.at[slot, kk], tm) * route[:, TOP_K + kk:TOP_K + kk + 1]
    x2 = x1_ref[...] + mod_ref[0, 5:6, :] * f
    o_ref[...] = _rms(x2) * gf_ref[...] if final else x2

    @pl.when(i + 2 < ntiles)
    def _():
        index_copy(dn_ref).wait()
        issue(lax.rem(i + 2, COMBINE_SLOTS))


def _combine(dest3, x1, route, modtab, g_final, yb, group_of_tile, final):
    t, d = x1.shape
    tm = TOKEN_TILE
    ntiles = t // tm
    n = dest3.shape[-1]
    row = lambda i: (i, 0)
    return pl.pallas_call(
        functools.partial(_combine_kernel, final=final, ntiles=ntiles),
        grid=(ntiles,),
        in_specs=[pl.BlockSpec((1, 1, n), lambda i: (0, 0, 0)),
                  pl.BlockSpec((1, 1, n), lambda i: (min(1, ntiles - 1), 0, 0)),
                  pl.BlockSpec((1, 1, n), lambda i: (jnp.minimum(i + 2, ntiles - 1), 0, 0)),
                  pl.BlockSpec((tm, d), row), pl.BlockSpec((tm, LANES), row),
                  pl.BlockSpec((1, 8, d), lambda i: (group_of_tile(i), 0, 0)),
                  pl.BlockSpec((1, d), lambda i: (0, 0)),
                  pl.BlockSpec(memory_space=pl.ANY)],
        out_specs=pl.BlockSpec((tm, d), row),
        out_shape=jax.ShapeDtypeStruct((t, d), F32),
        scratch_shapes=[pltpu.SMEM((n,), jnp.int32),
                        pltpu.VMEM((COMBINE_SLOTS, TOP_K, tm * ROW_TILES, LANES), F32),
                        pltpu.SemaphoreType.DMA, pltpu.SemaphoreType.DMA((COMBINE_SLOTS,))],
        compiler_params=_params("arbitrary"),
        name="moe_combine",
    )(dest3, dest3, dest3, x1, route, modtab, g_final.reshape(1, d), yb)


def _grid_swap(x, r1, r2):
    n, d = x.shape
    b = n // (r1 * r2)
    return x.reshape(b, r1, r2, d).swapaxes(1, 2).reshape(n, d)


def _prep_w_in(w_in):
    sizes = (QA_W, QA_W, VA_W, GATE_RANK, VA_W, QKVB_W, 2 * H_B, 2 * H_B, VB_W)
    offs = np.concatenate([[0], np.cumsum(sizes)])
    qa, ka, va, glr, oga, qkvb, ab, bb, zb = [w_in[..., offs[i]:offs[i + 1]] for i in range(len(sizes))]
    pad = jnp.zeros(w_in.shape[:-1] + (SM_W - GATE_RANK - 4 * H_B,), w_in.dtype)
    return jnp.concatenate([qa, ka, va, oga, zb, qkvb, glr, ab, bb, pad], axis=-1).astype(BF16)


def kernel(x_prompt, x_sample, state_gla, state_gdn, c, c_ctx, w_mod, b_mod, g_norm1, g_norm2, w_in,
           gla_w_gate2, gla_b_gate, gla_g_onorm, gdn_conv_w, gdn_a_log, gdn_dt_bias, gdn_g_onorm, w_out,
           router_w, router_b, w_gu, b_gu, w_down, b_down, g_final):
    bp, sp, d = x_prompt.shape
    bs, ns, _ = x_sample.shape
    depth = w_mod.shape[0]
    n_p, n_s = bp * sp, bs * ns
    t = n_p + n_s
    tm = TOKEN_TILE
    nsp, nss = SEQS_PER_STEP_SHORT, SEQS_PER_STEP_LONG
    assert d == D_MODEL and sp % tm == 0 and ns % tm == 0 and bs + 1 <= 8
    assert bp % nsp == 0 and bs % nss == 0 and n_p % (nss * ns) == 0
    rows_s = ns // GRID_W

    def group_of_tile(i):
        return jnp.where(i < n_p // tm, 0, 1 + (i - n_p // tm) // (ns // tm))

    cond = jnp.concatenate([c_ctx[None], c, jnp.zeros((8 - 1 - bs, d), F32)], axis=0)
    mods = _modulation(cond, w_mod, b_mod)
    mods = mods.reshape(depth, 8, N_MOD, d)[:, :1 + bs]
    modtab = jnp.concatenate([mods, jnp.zeros((depth, 1 + bs, 8 - N_MOD, d), F32)], axis=2)

    w_in_r = _prep_w_in(w_in)
    w_out_b = w_out.astype(BF16)
    wg = jnp.concatenate([gla_w_gate2, jnp.zeros((depth, 2, SM_W - GATE_RANK, QA_W), F32)], axis=2)
    wgh = wg.astype(BF16)
    wgl = (wg - wgh.astype(F32)).astype(BF16)
    bg = gla_b_gate.reshape(depth, 2, 1, QA_W)
    lane_pad = lambda v, off: jnp.pad(v.reshape(depth, 1, -1), ((0, 0), (0, 0), (off, SM_W - off - v[0].size)))
    a_log_vec = lane_pad(gdn_a_log, SM_A0)
    dtb_vec = lane_pad(gdn_dt_bias, SM_A0)
    conv_w = jnp.pad(gdn_conv_w, ((0, 0), (0, 8 - gdn_conv_w.shape[1]), (0, 0)))
    gon = jnp.stack([gla_g_onorm, gdn_g_onorm], axis=1)
    rw = jnp.pad(router_w, ((0, 0), (0, 0), (0, LANES - N_EXPERTS)))
    rb = jnp.pad(router_b, ((0, 0), (0, LANES - N_EXPERTS)), constant_values=-1e30).reshape(depth, 1, LANES)
    b_gu_r = jnp.concatenate([b_gu[..., 0::2], b_gu[..., 1::2]], axis=-1)

    hsel = jnp.eye(H_A, dtype=F32)
    st_a_s = jnp.einsum("bldhkv,hg->bldhvgk", state_gla.astype(F32), hsel).reshape(bs, depth, 2, VA_W, QA_W)
    st_b_s = state_gdn.astype(F32)
    st_a_0 = jnp.zeros((nsp, 2, VA_W, QA_W), F32)
    st_b_0 = jnp.zeros((nsp, 2, H_B, DK_B, DV_B), F32)

    n_blocks = (t * TOP_K) // EXPERT_BLOCK + N_EXPERTS
    cap = n_blocks * EXPERT_BLOCK
    tok_of_slot = jnp.repeat(jnp.arange(t, dtype=jnp.int32) * ROW_TILES, TOP_K)

    x = jnp.concatenate([x_prompt.reshape(n_p, d), x_sample.reshape(n_s, d)], axis=0)
    new_gla, new_gdn = [], []
    for l in range(depth):
        if l % 2 == 1:
            x = lax.dynamic_update_slice(x, _grid_swap(x[n_p:], rows_s, GRID_W), (n_p, 0))
        pa, gz, pb, sm = _pre_mixer(x, modtab[l], g_norm1[l], w_in_r[l], group_of_tile)
        oa_p, sa = _gla(pa, sm, st_a_0, wgh[l], wgl[l], bg[l], ntotal=bp, seqlen=sp, nseq=nsp, row_block0=0,
                        shared_state=True)
        oa_s, _ = _gla(pa, sm, st_a_s[:, l], wgh[l], wgl[l], bg[l], ntotal=bs, seqlen=ns, nseq=nss,
                       row_block0=n_p // (nss * ns), shared_state=False)
        ob_p, sb = _gdn(pb, sm, st_b_0, conv_w[l], a_log_vec[l], dtb_vec[l], ntotal=bp, seqlen=sp, nseq=nsp,
                        row_block0=0, shared_state=True)
        ob_s, _ = _gdn(pb, sm, st_b_s[:, l], conv_w[l], a_log_vec[l], dtb_vec[l], ntotal=bs, seqlen=ns, nseq=nss,
                       row_block0=n_p // (nss * ns), shared_state=False)
        new_gla.append(sa)
        new_gdn.append(sb)
        x1, h2t, route, counts = _post_mixer(x, oa_p, oa_s, ob_p, ob_s, gz, modtab[l], gon[l], w_out_b[l],
                                             g_norm2[l], rw[l], rb[l], group_of_tile)

        top_i = route[:, :TOP_K].astype(jnp.int32)
        rank = route[:, 2 * TOP_K:3 * TOP_K].astype(jnp.int32)
        sizes = counts[0, :N_EXPERTS].astype(jnp.int32)
        padded = (sizes + EXPERT_BLOCK - 1) // EXPERT_BLOCK * EXPERT_BLOCK
        pend = jnp.cumsum(padded)
        pstart = pend - padded
        onehot = top_i[..., None] == jnp.arange(N_EXPERTS, dtype=jnp.int32)
        dest = (rank + jnp.sum(jnp.where(onehot, pstart, 0), axis=-1)).reshape(-1)
        dest3 = (dest * ROW_TILES).reshape(t // tm, 1, tm * TOP_K)
        tok3 = jnp.zeros((cap,), jnp.int32).at[dest].set(tok_of_slot, unique_indices=True)
        tok3 = tok3.reshape(n_blocks, 1, EXPERT_BLOCK)
        nblk = (pend[-1] // EXPERT_BLOCK).reshape(1)
        blk_start = jnp.minimum(jnp.arange(n_blocks, dtype=jnp.int32), nblk[0] - 1) * EXPERT_BLOCK
        blk_e = jnp.sum(blk_start[:, None] >= pend[None, :], axis=-1).astype(jnp.int32)

        yb = _experts(l, blk_e, nblk, tok3, h2t, w_gu, b_gu_r[l], w_down, b_down[l])
        x = _combine(dest3, x1, route, modtab[l], g_final, yb, group_of_tile, final=(l == depth - 1))
        if l % 2 == 1:
            x = lax.dynamic_update_slice(x, _grid_swap(x[n_p:], GRID_W, rows_s), (n_p, 0))

    y_prompt = x[:n_p].reshape(bp, sp, d)
    y_sample = x[n_p:].reshape(bs, ns, d)
    sa_all = jnp.stack(new_gla, axis=1).reshape(bp, depth, 2, H_A, DV_A, H_A, DK_A)
    new_state_gla = jnp.einsum("bldhvgk,hg->bldhkv", sa_all, hsel)
    new_state_gdn = jnp.stack(new_gdn, axis=1)
    return (y_prompt, y_sample, new_state_gla, new_state_gdn)
```

```python
import functools

import numpy as np
import jax
import jax.numpy as jnp
from jax import lax
from jax.experimental import pallas as pl
from jax.experimental.pallas import tpu as pltpu

F32 = jnp.float32
BF16 = jnp.bfloat16

D_MODEL = 1024
GRID_W = 64
H_A, DK_A, DV_A = 4, 64, 128
H_B, DK_B, DV_B = 4, 128, 128
GATE_RANK = 16
GATE_NORMALIZER = 16.0
CHUNK = 64
N_EXPERTS = 32
TOP_K = 4
D_FF = 1024
SWIGLU_LIMIT = 7.0
SWIGLU_ALPHA = 1.702
N_MOD = 6
EPS = 1e-6

QA_W = H_A * DK_A
VA_W = H_A * DV_A
QB_W = H_B * DK_B
VB_W = H_B * DV_B
QKVB_W = 2 * QB_W + VB_W
PA_W = 2 * QA_W + VA_W
GZ_W = VA_W + VB_W
SM_W = 128
SM_A0 = GATE_RANK
SM_B0 = GATE_RANK + 2 * H_B
W_IN_COLS = PA_W + GZ_W + QKVB_W + SM_W

TOKEN_TILE = 256
EXPERT_BLOCK = 512
EXPERT_SLOTS = 3
COMBINE_SLOTS = 3
LANES = 128
SUBLANES = 8
ROW_TILES = D_MODEL // LANES
BF16_ROWS = 16
VMEM_LIMIT_BYTES = 56 * 2**20
SEQS_PER_STEP_SHORT = 4
SEQS_PER_STEP_LONG = 2

_NT = (((1,), (1,)), ((), ()))
_TN = (((0,), (0,)), ((), ()))

assert ROW_TILES == SUBLANES


def _params(*sem):
    return pltpu.CompilerParams(dimension_semantics=sem, vmem_limit_bytes=VMEM_LIMIT_BYTES)


def _dot(a, b):
    return jnp.dot(a.astype(BF16), b.astype(BF16), preferred_element_type=F32)


def _dot_nt(a, b):
    return lax.dot_general(a.astype(BF16), b.astype(BF16), _NT, preferred_element_type=F32)


def _dot_tn(a, b):
    return lax.dot_general(a.astype(BF16), b.astype(BF16), _TN, preferred_element_type=F32)


def _split(x):
    hi = x.astype(BF16)
    lo = (x - hi.astype(F32)).astype(BF16)
    return hi, lo


def _dot_x2(m, x):
    hi, lo = _split(x)
    return _dot(m, hi) + _dot(m, lo)


def _dot_2x(x, m):
    hi, lo = _split(x)
    return _dot(hi, m) + _dot(lo, m)


def _dot3(a, b):
    ah, al = _split(a)
    bh, bl = _split(b)
    return _dot(ah, bh) + _dot(ah, bl) + _dot(al, bh)


def _sigmoid(x):
    return 1.0 / (1.0 + jnp.exp(-x))


def _softplus(x):
    return jnp.maximum(x, 0.0) + jnp.log(1.0 + jnp.exp(-jnp.abs(x)))


def _rms(x):
    return x * lax.rsqrt(jnp.mean(x * x, axis=-1, keepdims=True) + EPS)


def _stack4(x, mask):
    xb = x.astype(BF16)
    return jnp.concatenate([xb, xb, xb, xb], axis=0) * mask


def _to_rows(tiles_ref, n):
    return jnp.concatenate([tiles_ref[pl.ds(j, n, stride=ROW_TILES), :] for j in range(ROW_TILES)], axis=1)


def _store_rows(tiles_ref, x):
    n = x.shape[0]
    for j in range(ROW_TILES):
        tiles_ref[pl.ds(j, n, stride=ROW_TILES), :] = x[:, j * LANES:(j + 1) * LANES]


def _block_mask(rows, row_blk, cols, col_blk):
    r = np.arange(rows)[:, None] // row_blk
    c = np.arange(cols)[None, :] // col_blk
    return (r == c).astype(np.float32)


@functools.lru_cache(maxsize=None)
def _gla_consts():
    c = CHUNK
    w = np.zeros((7, c, c), np.float32)
    m = np.zeros((7, c, c), np.float32)
    w[0] = np.tril(np.ones((c, c)))
    m[0] = np.eye(c)
    i = np.arange(c)
    for l, s in enumerate((32, 16, 8, 4, 2, 1), start=1):
        ref = (i // (2 * s)) * 2 * s + s - 1
        for r in range(c):
            if r > ref[r]:
                w[l, r, ref[r] + 1:r + 1] = 1.0
            else:
                w[l, r, r + 1:ref[r] + 1] = 1.0
        same = (i[:, None] // (2 * s)) == (i[None, :] // (2 * s))
        m[l] = same & ((i[:, None] % (2 * s)) >= s) & ((i[None, :] % (2 * s)) < s)
    assert np.array_equal(m.sum(0), np.tril(np.ones((c, c))))
    wall = np.stack([w.reshape(7 * c, c), w[:, ::-1, ::-1].reshape(7 * c, c)])
    masks = np.stack([np.tile(m, (1, 1, H_A)), np.tile(m[:, ::-1, ::-1], (1, 1, H_A))])
    return dict(
        wall=jnp.asarray(wall, BF16),
        masks=jnp.asarray(masks, F32),
        hm=jnp.asarray(_block_mask(H_A * c, c, QA_W, DK_A), BF16),
        vm=jnp.asarray(_block_mask(H_A * c, c, VA_W, DV_A), BF16),
        bd=jnp.asarray(_block_mask(VA_W, DV_A, QA_W, DK_A), F32),
    )


@functools.lru_cache(maxsize=None)
def _gdn_consts():
    c = CHUNK
    tril = np.tril(np.ones((c, c), np.float32))
    lc = np.stack([tril, tril.T])
    ut = np.stack([np.tile(lc[d].T, (1, H_B)) for d in range(2)])
    cm = np.stack([np.tile(lc[d], (1, H_B)) for d in range(2)])
    eye = np.tile(np.eye(c, dtype=np.float32), (1, H_B))
    sm = cm - eye[None]
    i = np.arange(c)
    dblk = np.tile(((i[:, None] // 16) == (i[None, :] // 16)).astype(np.float32), (1, H_B))
    eg = np.zeros((2, SM_W, H_B * c), np.float32)
    eb = np.zeros((2, SM_W, H_B * c), np.float32)
    egw = np.zeros((2, SM_W, QB_W), np.float32)
    ebw = np.zeros((2, SM_W, QB_W), np.float32)
    for d in range(2):
        for h in range(H_B):
            eg[d, SM_A0 + d * H_B + h, h * c:(h + 1) * c] = 1.0
            eb[d, SM_B0 + d * H_B + h, h * c:(h + 1) * c] = 1.0
            egw[d, SM_A0 + d * H_B + h, h * DK_B:(h + 1) * DK_B] = 1.0
            ebw[d, SM_B0 + d * H_B + h, h * DK_B:(h + 1) * DK_B] = 1.0
    return dict(
        lc=jnp.asarray(lc, BF16), ut=jnp.asarray(ut, F32), cm=jnp.asarray(cm, F32), sm=jnp.asarray(sm, F32),
        eye=jnp.asarray(eye, F32), dblk=jnp.asarray(dblk, F32),
        eg=jnp.asarray(eg, BF16), eb=jnp.asarray(eb, BF16), egw=jnp.asarray(egw, BF16), ebw=jnp.asarray(ebw, BF16),
        hm=jnp.asarray(_block_mask(H_B * c, c, H_B * c, c), BF16),
        km=jnp.asarray(_block_mask(H_B * c, c, QB_W, DK_B), BF16),
    )


@functools.lru_cache(maxsize=None)
def _deinterleave_matrix():
    p = np.zeros((2 * LANES, 2 * LANES), np.float32)
    j = np.arange(LANES)
    p[2 * j, j] = 1.0
    p[2 * j + 1, LANES + j] = 1.0
    return jnp.asarray(p, BF16)


def _mod_kernel(c_ref, w_ref, b_ref, o_ref):
    c = c_ref[...]
    o_ref[0] = _dot3(c * _sigmoid(c), w_ref[0]) + b_ref[0]


def _modulation(cond, w_mod, b_mod):
    depth, d, n = w_mod.shape
    tn = 1536
    return pl.pallas_call(
        _mod_kernel,
        grid=(depth, n // tn),
        in_specs=[pl.BlockSpec((8, d), lambda l, j: (0, 0)),
                  pl.BlockSpec((1, d, tn), lambda l, j: (l, 0, j)),
                  pl.BlockSpec((1, 1, tn), lambda l, j: (l, 0, j))],
        out_specs=pl.BlockSpec((1, 8, tn), lambda l, j: (l, 0, j)),
        out_shape=jax.ShapeDtypeStruct((depth, 8, n), F32),
        compiler_params=_params("parallel", "parallel"),
        name="modulation",
    )(cond, w_mod, b_mod.reshape(depth, 1, n))


def _pre_kernel(x_ref, mod_ref, g_ref, w_ref, pa_ref, gz_ref, pb_ref, sm_ref):
    h = _rms(x_ref[...]) * g_ref[...]
    h = h * (1.0 + mod_ref[0, 1:2, :]) + mod_ref[0, 0:1, :]
    p = jnp.dot(h.astype(BF16), w_ref[...], preferred_element_type=F32)
    pa_ref[...] = p[:, :PA_W].astype(BF16)
    gz_ref[...] = p[:, PA_W:PA_W + GZ_W].astype(BF16)
    pb_ref[...] = p[:, PA_W + GZ_W:PA_W + GZ_W + QKVB_W].astype(BF16)
    sm_ref[...] = p[:, PA_W + GZ_W + QKVB_W:]


def _pre_mixer(x, modtab, g1, w_in_r, group_of_tile):
    t, d = x.shape
    tm = TOKEN_TILE
    row = lambda i: (i, 0)
    return pl.pallas_call(
        _pre_kernel,
        grid=(t // tm,),
        in_specs=[pl.BlockSpec((tm, d), row),
                  pl.BlockSpec((1, 8, d), lambda i: (group_of_tile(i), 0, 0)),
                  pl.BlockSpec((1, d), lambda i: (0, 0)),
                  pl.BlockSpec((d, W_IN_COLS), lambda i: (0, 0))],
        out_specs=[pl.BlockSpec((tm, PA_W), row), pl.BlockSpec((tm, GZ_W), row),
                   pl.BlockSpec((tm, QKVB_W), row), pl.BlockSpec((tm, SM_W), row)],
        out_shape=[jax.ShapeDtypeStruct((t, PA_W), BF16), jax.ShapeDtypeStruct((t, GZ_W), BF16),
                   jax.ShapeDtypeStruct((t, QKVB_W), BF16), jax.ShapeDtypeStruct((t, SM_W), F32)],
        compiler_params=_params("parallel"),
        name="pre_mixer",
    )(x, modtab, g1.reshape(1, d), w_in_r)


def _gla_kernel(pa_ref, sm_ref, wall_ref, mask_ref, wgh_ref, wgl_ref, bg_ref, hm_ref, vm_ref, bd_ref, st0_ref,
                o_ref, stf_ref, st_scr, *, nchunk, nseq):
    c = CHUNK
    seqlen = nchunk * c
    hm = hm_ref[...]
    vm = vm_ref[...]
    bd = bd_ref[...]
    st_scr[...] = st0_ref[...]

    chains = [(s, d) for s in range(nseq) for d in range(2)]
    dirs = [d for _, d in chains]

    def body(ci, carry):
        rows = [pl.ds(pl.multiple_of(s * seqlen + (ci if d == 0 else nchunk - 1 - ci) * c, c), c)
                for s, d in chains]
        q = [pa_ref[r, 0:QA_W].astype(F32) * (DK_A ** -0.5) for r in rows]
        k = [pa_ref[r, QA_W:2 * QA_W].astype(F32) for r in rows]
        v = [pa_ref[r, 2 * QA_W:PA_W] for r in rows]
        sm = [_split(sm_ref[r, :]) for r in rows]
        x = _each(lambda y, d: _dot(y[0], wgh_ref[d]) + _dot(y[0], wgl_ref[d]) + _dot(y[1], wgh_ref[d]) + bg_ref[d],
                  sm, dirs)
        gk = [(jnp.minimum(y, 0.0) - jnp.log(1.0 + jnp.exp(-jnp.abs(y)))) * (1.0 / GATE_NORMALIZER) for y in x]
        e_all = _each(lambda g, d: _dot_x2(wall_ref[d], g), gk, dirs)
        a = None
        for l in range(7):
            if l == 0:
                qt, kt = q, k
            else:
                f = [jnp.exp(e[l * c:(l + 1) * c]) for e in e_all]
                qt = _each(lambda y, z: y * z, q, f)
                kt = _each(lambda y, z: y * z, k, f)
            r = _each(lambda y, z, d: _dot_nt(y, _stack4(z, hm)) * mask_ref[d, l], qt, kt, dirs)
            a = r if a is None else _each(lambda y, z: y + z, a, r)
        b = [e[0:c] for e in e_all]
        o = _each(lambda aa, vv, qq, bb, sd: _dot(aa, _stack4(vv, vm)) + _dot_nt(qq * jnp.exp(bb), st_scr[sd]),
                  a, v, q, b, chains)
        for r, d, y in zip(rows, dirs, o):
            o_ref[r, d * VA_W:(d + 1) * VA_W] = y
        b_last = _each(lambda bb, d: bb[c - 1:c] if d == 0 else bb[0:1], b, dirs)
        upd = _each(lambda vv, kk, bl, bb: _dot_tn(vv, kk * jnp.exp(bl - bb)), v, k, b_last, b)
        for sd, bl, up in zip(chains, b_last, upd):
            st_scr[sd] = st_scr[sd] * jnp.exp(bl) + up * bd
        return carry

    lax.fori_loop(0, nchunk, body, 0)
    stf_ref[...] = st_scr[...]


def _gla(pa, sm, st0, wgh, wgl, bg, *, ntotal, seqlen, nseq, row_block0, shared_state):
    cst = _gla_consts()
    nchunk = seqlen // CHUNK
    rows = nseq * seqlen
    seq = lambda b: (row_block0 + b, 0)
    full = lambda *shape: pl.BlockSpec(shape, lambda b: (0,) * len(shape))
    st_in = (lambda b: (0, 0, 0, 0)) if shared_state else (lambda b: (b, 0, 0, 0))
    return pl.pallas_call(
        functools.partial(_gla_kernel, nchunk=nchunk, nseq=nseq),
        grid=(ntotal // nseq,),
        in_specs=[pl.BlockSpec((rows, PA_W), seq), pl.BlockSpec((rows, SM_W), seq),
                  full(2, 7 * CHUNK, CHUNK), full(2, 7, CHUNK, QA_W),
                  full(2, SM_W, QA_W), full(2, SM_W, QA_W), full(2, 1, QA_W),
                  full(QA_W, QA_W), full(QA_W, VA_W), full(VA_W, QA_W),
                  pl.BlockSpec((nseq, 2, VA_W, QA_W), st_in)],
        out_specs=[pl.BlockSpec((rows, 2 * VA_W), lambda b: (b, 0)),
                   pl.BlockSpec((nseq, 2, VA_W, QA_W), lambda b: (b, 0, 0, 0))],
        out_shape=[jax.ShapeDtypeStruct((ntotal * seqlen, 2 * VA_W), F32),
                   jax.ShapeDtypeStruct((ntotal, 2, VA_W, QA_W), F32)],
        scratch_shapes=[pltpu.VMEM((nseq, 2, VA_W, QA_W), F32)],
        compiler_params=_params("parallel"),
        name="gla_l%d" % seqlen,
    )(pa, sm, cst["wall"], cst["masks"], wgh, wgl, bg, cst["hm"], cst["vm"], cst["bd"], st0)


def _each(f, *lists):
    return [f(*xs) for xs in zip(*lists)]


def _mm_heads(xs, ys, hm):
    return _each(lambda x, y: _dot(x, _stack4(y, hm)), xs, ys)


def _unit_tri_inverse(ms, eye, dblk, hm):
    mm = functools.partial(_mm_heads, hm=hm)
    dg = [m * dblk for m in ms]
    lo = _each(lambda m, g: m - g, ms, dg)
    d2 = mm(dg, dg)
    d3 = mm(dg, d2)
    d4 = mm(d2, d2)
    p1 = _each(lambda g, a, b: eye - g + a - b, dg, d2, d3)
    p2 = _each(lambda a, b: a + b, p1, mm(p1, d4))
    d8 = mm(d4, d4)
    td = _each(lambda a, b: a + b, p2, mm(p2, d8))
    e1 = mm(td, lo)
    e2 = mm(e1, e1)
    e3 = mm(e1, e2)
    return mm(_each(lambda a, b, c: eye - a + b - c, e1, e2, e3), td)


def _gdn_kernel(pb_ref, sm_ref, cw_ref, al_ref, dtb_ref, lc_ref, ut_ref, cm_ref, smk_ref, eye_ref, dblk_ref,
                eg_ref, eb_ref, egw_ref, ebw_ref, hm_ref, km_ref, st0_ref, o_ref, stf_ref, qkv_scr, s_scr,
                *, nchunk, nseq):
    c = CHUNK
    seqlen = nchunk * c
    w0 = cw_ref[0:1, :]
    w1 = cw_ref[1:2, :]
    w2 = cw_ref[2:3, :]

    def conv_body(cc, carry):
        for s in range(nseq):
            r0 = pl.multiple_of(s * seqlen + cc * c, c)
            x = pb_ref[pl.ds(r0, c), :].astype(F32)
            p0 = pl.multiple_of(jnp.maximum(r0 - BF16_ROWS, 0), BF16_ROWS)
            n0 = pl.multiple_of(jnp.minimum(r0 + c, nseq * seqlen - BF16_ROWS), BF16_ROWS)
            pg = pb_ref[pl.ds(p0, BF16_ROWS), :]
            ng = pb_ref[pl.ds(n0, BF16_ROWS), :]
            prow = jnp.where(cc > 0, pg.astype(F32)[BF16_ROWS - 1:BF16_ROWS], 0.0)
            nrow = jnp.where(cc < nchunk - 1, ng.astype(F32)[0:1], 0.0)
            rid = lax.broadcasted_iota(jnp.int32, x.shape, 0)
            xp = jnp.where(rid == 0, prow, pltpu.roll(x, 1, axis=0))
            xn = jnp.where(rid == c - 1, nrow, pltpu.roll(x, c - 1, axis=0))
            y = w0 * xp + w1 * x + w2 * xn
            sl = y * _sigmoid(y)
            parts = []
            for h in range(2 * H_B):
                p = sl[:, h * DK_B:(h + 1) * DK_B]
                nrm = lax.rsqrt(jnp.sum(p * p, axis=-1, keepdims=True) + EPS)
                parts.append(p * (nrm * (DK_B ** -0.5 if h < H_B else 1.0)))
            parts.append(sl[:, 2 * QB_W:])
            qkv_scr[pl.ds(r0, c), :] = jnp.concatenate(parts, axis=1).astype(BF16)
        return carry

    lax.fori_loop(0, nchunk, conv_body, 0)

    hm = hm_ref[...]
    km = km_ref[...]
    eye = eye_ref[...]
    dblk = dblk_ref[...]
    ones = jnp.ones((c, c), BF16)
    neg_a = -jnp.exp(al_ref[...])
    dtb = dtb_ref[...]
    s_scr[...] = st0_ref[...]

    chains = [(s, d) for s in range(nseq) for d in range(2)]
    dirs = [d for _, d in chains]
    heads = [slice(h * DK_B, (h + 1) * DK_B) for h in range(H_B)]

    def body(ci, carry):
        rows = [pl.ds(pl.multiple_of(s * seqlen + (ci if d == 0 else nchunk - 1 - ci) * c, c), c)
                for s, d in chains]
        q = [qkv_scr[r, 0:QB_W].astype(F32) for r in rows]
        k = [qkv_scr[r, QB_W:2 * QB_W].astype(F32) for r in rows]
        v = [qkv_scr[r, 2 * QB_W:QKVB_W].astype(F32) for r in rows]
        sm = [sm_ref[r, :] for r in rows]
        g_all = [_split(neg_a * _softplus(x + dtb)) for x in sm]
        beta_all = [_split(_sigmoid(x)) for x in sm]
        two = lambda ref: (lambda x, d: _dot(x[0], ref[d]) + _dot(x[1], ref[d]))
        g_b = _each(two(eg_ref), g_all, dirs)
        g_w = _each(two(egw_ref), g_all, dirs)
        beta_i = _each(two(eb_ref), beta_all, dirs)
        beta_w = _each(two(ebw_ref), beta_all, dirs)
        gam_i = _each(lambda x, d: _dot_x2(lc_ref[d], x), g_b, dirs)
        gam_j = _each(lambda x, d: _dot_x2(ones, x * ut_ref[d]), g_b, dirs)
        gam_w = _each(lambda x, d: _dot_x2(lc_ref[d], x), g_w, dirs)
        dec = _each(lambda a, b, d: jnp.exp(jnp.minimum(a - b, 0.0)) * cm_ref[d], gam_i, gam_j, dirs)
        kkqk = _each(lambda kk, qq: _dot_nt(jnp.concatenate([kk, qq], axis=0), _stack4(kk, km)), k, q)
        m = _each(lambda bi, x, dc, d: smk_ref[d] * bi * x[:c] * dc, beta_i, kkqk, dec, dirs)
        t_inv = _unit_tri_inverse(m, eye, dblk, hm)
        eg_w = [jnp.exp(x) for x in gam_w]
        kh = _each(lambda kk, bw, e: kk * (bw * e), k, beta_w, eg_w)
        qh = _each(lambda qq, e: qq * e, q, eg_w)
        res = [[_dot(jnp.concatenate([a[:, hs], b[:, hs]], axis=0), s_scr[s, d, h])
                for h, hs in enumerate(heads)] for a, b, (s, d) in zip(kh, qh, chains)]
        rhs = _each(lambda vv, bw, r: vv * bw - jnp.concatenate([x[:c] for x in r], axis=1), v, beta_w, res)
        u = _each(lambda t, r: _dot(t, _stack4(r, km)), t_inv, rhs)
        o = _each(lambda r, x, dc, uu: jnp.concatenate([y[c:] for y in r], axis=1)
                  + _dot(x[c:] * dc, _stack4(uu, km)), res, kkqk, dec, u)
        for r, d, x in zip(rows, dirs, o):
            o_ref[r, d * VB_W:(d + 1) * VB_W] = x
        g_last = _each(lambda x, d: x[c - 1:c] if d == 0 else x[0:1], gam_w, dirs)
        kd = _each(lambda kk, gl, gw: kk * jnp.exp(gl - gw), k, g_last, gam_w)
        upd = [[_dot_tn(a[:, hs], b[:, hs]) for hs in heads] for a, b in zip(kd, u)]
        for (s, d), gl, up in zip(chains, g_last, upd):
            for h, hs in enumerate(heads):
                s_scr[s, d, h] = s_scr[s, d, h] * jnp.exp(gl[:, hs]) + up[h]
        return carry

    lax.fori_loop(0, nchunk, body, 0)
    stf_ref[...] = s_scr[...]


def _gdn(pb, sm, st0, conv_w, a_log_vec, dtb_vec, *, ntotal, seqlen, nseq, row_block0, shared_state):
    cst = _gdn_consts()
    nchunk = seqlen // CHUNK
    rows = nseq * seqlen
    hc = H_B * CHUNK
    seq = lambda b: (row_block0 + b, 0)
    full = lambda *shape: pl.BlockSpec(shape, lambda b: (0,) * len(shape))
    st_in = (lambda b: (0, 0, 0, 0, 0)) if shared_state else (lambda b: (b, 0, 0, 0, 0))
    return pl.pallas_call(
        functools.partial(_gdn_kernel, nchunk=nchunk, nseq=nseq),
        grid=(ntotal // nseq,),
        in_specs=[pl.BlockSpec((rows, QKVB_W), seq), pl.BlockSpec((rows, SM_W), seq),
                  full(8, QKVB_W), full(1, SM_W), full(1, SM_W),
                  full(2, CHUNK, CHUNK), full(2, CHUNK, hc), full(2, CHUNK, hc), full(2, CHUNK, hc),
                  full(CHUNK, hc), full(CHUNK, hc),
                  full(2, SM_W, hc), full(2, SM_W, hc), full(2, SM_W, QB_W), full(2, SM_W, QB_W),
                  full(hc, hc), full(hc, QB_W),
                  pl.BlockSpec((nseq, 2, H_B, DK_B, DV_B), st_in)],
        out_specs=[pl.BlockSpec((rows, 2 * VB_W), lambda b: (b, 0)),
                   pl.BlockSpec((nseq, 2, H_B, DK_B, DV_B), lambda b: (b, 0, 0, 0, 0))],
        out_shape=[jax.ShapeDtypeStruct((ntotal * seqlen, 2 * VB_W), F32),
                   jax.ShapeDtypeStruct((ntotal, 2, H_B, DK_B, DV_B), F32)],
        scratch_shapes=[pltpu.VMEM((rows, QKVB_W), BF16), pltpu.VMEM((nseq, 2, H_B, DK_B, DV_B), F32)],
        compiler_params=_params("parallel"),
        name="gdn_l%d" % seqlen,
    )(pb, sm, conv_w, a_log_vec, dtb_vec, cst["lc"], cst["ut"], cst["cm"], cst["sm"], cst["eye"], cst["dblk"],
      cst["eg"], cst["eb"], cst["egw"], cst["ebw"], cst["hm"], cst["km"], st0)


def _post_kernel(x_ref, oap_ref, oas_ref, obp_ref, obs_ref, gz_ref, mod_ref, gon_ref, wo_ref, g2_ref, rw_ref,
                 rb_ref, ls_ref, x1_ref, h2_ref, route_ref, cnt_ref, *, context_tiles):
    is_context = pl.program_id(0) < context_tiles

    @pl.when(pl.program_id(0) == 0)
    def _():
        cnt_ref[...] = jnp.zeros_like(cnt_ref)

    gz = gz_ref[...]
    heads = []
    for grp, (op_ref, os_ref) in enumerate(((oap_ref, oas_ref), (obp_ref, obs_ref))):
        o2 = jnp.where(is_context, op_ref[...], os_ref[...])
        o = o2[:, :VA_W] + o2[:, VA_W:]
        for h in range(H_A):
            hs = slice(h * DV_A, (h + 1) * DV_A)
            gate = gz[:, grp * VA_W + h * DV_A:grp * VA_W + (h + 1) * DV_A].astype(F32)
            heads.append(_rms(o[:, hs]) * gon_ref[grp:grp + 1, :] * (gate * _sigmoid(gate)))
    mixed = jnp.concatenate(heads, axis=1)
    y = jnp.dot(mixed.astype(BF16), wo_ref[...], preferred_element_type=F32)
    x1 = x_ref[...] + mod_ref[0, 2:3, :] * y
    x1_ref[...] = x1
    h2 = _rms(x1) * g2_ref[...]
    h2 = h2 * (1.0 + mod_ref[0, 4:5, :]) + mod_ref[0, 3:4, :]
    _store_rows(h2_ref, h2)

    logits = _dot3(h2, rw_ref[...]) + rb_ref[...]
    lane = lax.broadcasted_iota(jnp.int32, logits.shape, 1).astype(F32)
    vals, idxs = [], []
    rem = logits
    for _ in range(TOP_K):
        mx = jnp.max(rem, axis=-1, keepdims=True)
        ix = jnp.min(jnp.where(rem == mx, lane, float(LANES)), axis=-1, keepdims=True)
        vals.append(mx)
        idxs.append(ix)
        rem = jnp.where(lane == ix, -jnp.inf, rem)
    exps = [jnp.exp(v - vals[0]) for v in vals]
    inv = 1.0 / (exps[0] + exps[1] + exps[2] + exps[3])
    onehot = jnp.zeros(logits.shape, F32)
    for ix in idxs:
        onehot = onehot + (lane == ix).astype(F32)
    before = _dot(ls_ref[...], onehot) + cnt_ref[...]
    route = jnp.zeros(logits.shape, F32)
    for kk in range(TOP_K):
        rank = jnp.sum(jnp.where(lane == idxs[kk], before, 0.0), axis=-1, keepdims=True)
        route = jnp.where(lane == kk, idxs[kk], route)
        route = jnp.where(lane == TOP_K + kk, exps[kk] * inv, route)
        route = jnp.where(lane == 2 * TOP_K + kk, rank, route)
    route_ref[...] = route
    cnt_ref[...] = cnt_ref[...] + jnp.sum(onehot, axis=0, keepdims=True)


def _post_mixer(x, oa_p, oa_s, ob_p, ob_s, gz, modtab, gon, w_out, g2, rw, rb, group_of_tile):
    t, d = x.shape
    tm = TOKEN_TILE
    context_tiles = oa_p.shape[0] // tm
    row = lambda i: (i, 0)
    ctx_row = lambda i: (jnp.minimum(i, context_tiles - 1), 0)
    lat_row = lambda i: (jnp.maximum(i - context_tiles, 0), 0)
    const = lambda i: (0, 0)
    lstrict = jnp.asarray(np.tril(np.ones((tm, tm), np.float32), -1), BF16)
    return pl.pallas_call(
        functools.partial(_post_kernel, context_tiles=context_tiles),
        grid=(t // tm,),
        in_specs=[pl.BlockSpec((tm, d), row),
                  pl.BlockSpec((tm, 2 * VA_W), ctx_row), pl.BlockSpec((tm, 2 * VA_W), lat_row),
                  pl.BlockSpec((tm, 2 * VB_W), ctx_row), pl.BlockSpec((tm, 2 * VB_W), lat_row),
                  pl.BlockSpec((tm, GZ_W), row),
                  pl.BlockSpec((1, 8, d), lambda i: (group_of_tile(i), 0, 0)),
                  pl.BlockSpec((2, DV_A), const), pl.BlockSpec((d, d), const), pl.BlockSpec((1, d), const),
                  pl.BlockSpec((d, LANES), const), pl.BlockSpec((1, LANES), const),
                  pl.BlockSpec((tm, tm), const)],
        out_specs=[pl.BlockSpec((tm, d), row), pl.BlockSpec((tm * ROW_TILES, LANES), row),
                   pl.BlockSpec((tm, LANES), row), pl.BlockSpec((1, LANES), const)],
        out_shape=[jax.ShapeDtypeStruct((t, d), F32), jax.ShapeDtypeStruct((t * ROW_TILES, LANES), F32),
                   jax.ShapeDtypeStruct((t, LANES), F32), jax.ShapeDtypeStruct((1, LANES), F32)],
        compiler_params=_params("arbitrary"),
        name="post_mixer",
    )(x, oa_p, oa_s, ob_p, ob_s, gz, modtab, gon, w_out, g2.reshape(1, d), rw, rb, lstrict)


def _tile_copy(src, src_row, dst, dst_row, sem):
    return pltpu.make_async_copy(src.at[pl.ds(pl.multiple_of(src_row, SUBLANES), SUBLANES)],
                                 dst.at[pl.ds(pl.multiple_of(dst_row, SUBLANES), SUBLANES)], sem)


def _expert_kernel(be_ref, nb_ref, tok0_ref, tok1_ref, tokn_ref, h2_hbm, wgu_ref, bgu_ref, wd_ref, bd_ref,
                   p_ref, y_ref, wgu_s, wd_s, xbuf, idx_smem, idx_sem, sems):
    i = pl.program_id(0)
    nb = nb_ref[0]
    bm = EXPERT_BLOCK
    slot = lax.rem(i, EXPERT_SLOTS)

    def index_copy(tok_ref):
        return pltpu.make_async_copy(tok_ref.at[0, 0], idx_smem, idx_sem)

    def issue(into):
        for r in range(bm):
            _tile_copy(h2_hbm, idx_smem[r], xbuf.at[into], r * SUBLANES, sems.at[into]).start()

    @pl.when(i == 0)
    def _():
        first = index_copy(tok0_ref)
        first.start()
        first.wait()
        issue(0)

    @pl.when(jnp.logical_and(i == 0, 1 < nb))
    def _():
        second = index_copy(tok1_ref)
        second.start()
        second.wait()
        issue(1)

    @pl.when(i + 2 < nb)
    def _():
        index_copy(tokn_ref).start()

    e = be_ref[i]
    prev = be_ref[jnp.maximum(i - 1, 0)]

    @pl.when(jnp.logical_or(i == 0, e != prev))
    def _():
        p = p_ref[...]
        for j in range(2 * D_FF // (2 * LANES)):
            w = wgu_ref[0, 0, :, j * 2 * LANES:(j + 1) * 2 * LANES].astype(BF16)
            r = jnp.dot(w, p, preferred_element_type=F32).astype(BF16)
            wgu_s[:, j * LANES:(j + 1) * LANES] = r[:, :LANES]
            wgu_s[:, D_FF + j * LANES:D_FF + (j + 1) * LANES] = r[:, LANES:]
        wd_s[...] = wd_ref[0, 0].astype(BF16)

    @pl.when(i < nb)
    def _():
        for r in range(bm):
            _tile_copy(h2_hbm, 0, xbuf.at[slot], 0, sems.at[slot]).wait()
        x = _to_rows(xbuf.at[slot], bm)
        gu = jnp.dot(x.astype(BF16), wgu_s[...], preferred_element_type=F32) + bgu_ref[0]
        gate = jnp.minimum(gu[:, :D_FF], SWIGLU_LIMIT)
        up = jnp.clip(gu[:, D_FF:], -SWIGLU_LIMIT, SWIGLU_LIMIT)
        hdn = (up + 1.0) * (gate * _sigmoid(SWIGLU_ALPHA * gate))
        _store_rows(y_ref, jnp.dot(hdn.astype(BF16), wd_s[...], preferred_element_type=F32) + bd_ref[0])

    @pl.when(i >= nb)
    def _():
        y_ref[...] = jnp.zeros_like(y_ref)

    @pl.when(i + 2 < nb)
    def _():
        index_copy(tokn_ref).wait()
        issue(lax.rem(i + 2, EXPERT_SLOTS))


def _experts(layer, blk_e, nblk, tok3, h2t, w_gu, b_gu_r, w_down, b_down):
    n_blocks = tok3.shape[0]
    bm = EXPERT_BLOCK
    d = D_MODEL
    n_e, n_gu = w_gu.shape[1], w_gu.shape[3]
    w4 = lambda i, be, nb: (layer, be[i], 0, 0)
    b3 = lambda i, be, nb: (be[i], 0, 0)
    grid_spec = pltpu.PrefetchScalarGridSpec(
        num_scalar_prefetch=2,
        grid=(n_blocks,),
        in_specs=[pl.BlockSpec((1, 1, bm), lambda i, be, nb: (0, 0, 0)),
                  pl.BlockSpec((1, 1, bm), lambda i, be, nb: (1, 0, 0)),
                  pl.BlockSpec((1, 1, bm), lambda i, be, nb: (jnp.minimum(i + 2, n_blocks - 1), 0, 0)),
                  pl.BlockSpec(memory_space=pl.ANY),
                  pl.BlockSpec((1, 1, d, n_gu), w4), pl.BlockSpec((1, 1, n_gu), b3),
                  pl.BlockSpec((1, 1, D_FF, d), w4), pl.BlockSpec((1, 1, d), b3),
                  pl.BlockSpec((2 * LANES, 2 * LANES), lambda i, be, nb: (0, 0))],
        out_specs=pl.BlockSpec((bm * ROW_TILES, LANES), lambda i, be, nb: (i, 0)),
        scratch_shapes=[pltpu.VMEM((d, n_gu), BF16), pltpu.VMEM((D_FF, d), BF16),
                        pltpu.VMEM((EXPERT_SLOTS, bm * ROW_TILES, LANES), F32), pltpu.SMEM((bm,), jnp.int32),
                        pltpu.SemaphoreType.DMA, pltpu.SemaphoreType.DMA((EXPERT_SLOTS,))],
    )
    return pl.pallas_call(
        _expert_kernel,
        grid_spec=grid_spec,
        out_shape=jax.ShapeDtypeStruct((n_blocks * bm * ROW_TILES, LANES), F32),
        compiler_params=_params("arbitrary"),
        name="moe_experts",
    )(blk_e, nblk, tok3, tok3, tok3, h2t, w_gu, b_gu_r.reshape(n_e, 1, n_gu), w_down, b_down.reshape(n_e, 1, d),
      _deinterleave_matrix())


def _combine_kernel(d0_ref, d1_ref, dn_ref, x1_ref, route_ref, mod_ref, gf_ref, yb_hbm, o_ref, idx_smem, rows,
                    idx_sem, sems, *, final, ntiles):
    i = pl.program_id(0)
    tm = x1_ref.shape[0]
    n = tm * TOP_K
    slot = lax.rem(i, COMBINE_SLOTS)

    def index_copy(dest_ref):
        return pltpu.make_async_copy(dest_ref.at[0, 0], idx_smem, idx_sem)

    def issue(into):
        for j in range(n):
            _tile_copy(yb_hbm, idx_smem[j], rows.at[into, j % TOP_K], (j // TOP_K) * SUBLANES,
                       sems.at[into]).start(priority=j % 2)

    @pl.when(i == 0)
    def _():
        for tile, dest_ref in enumerate((d0_ref, d1_ref)[:min(2, ntiles)]):
            load = index_copy(dest_ref)
            load.start()
            load.wait()
            issue(tile)

    @pl.when(i + 2 < ntiles)
    def _():
        index_copy(dn_ref).start()

    for j in range(n):
        _tile_copy(yb_hbm, 0, rows.at[slot, 0], 0, sems.at[slot]).wait()
    route = route_ref[...]
    f = _to_rows(rows.at[slot, 0], tm) * route[:, TOP_K:TOP_K + 1]
    for kk in range(1, TOP_K):
        f = f + _to_rows(rows.at[slot, kk], tm) * route[:, TOP_K + kk:TOP_K + kk + 1]
    x2 = x1_ref[...] + mod_ref[0, 5:6, :] * f
    o_ref[...] = _rms(x2) * gf_ref[...] if final else x2

    @pl.when(i + 2 < ntiles)
    def _():
        index_copy(dn_ref).wait()
        issue(lax.rem(i + 2, COMBINE_SLOTS))


def _combine(dest3, x1, route, modtab, g_final, yb, group_of_tile, final):
    t, d = x1.shape
    tm = TOKEN_TILE
    ntiles = t // tm
    n = dest3.shape[-1]
    row = lambda i: (i, 0)
    return pl.pallas_call(
        functools.partial(_combine_kernel, final=final, ntiles=ntiles),
        grid=(ntiles,),
        in_specs=[pl.BlockSpec((1, 1, n), lambda i: (0, 0, 0)),
                  pl.BlockSpec((1, 1, n), lambda i: (min(1, ntiles - 1), 0, 0)),
                  pl.BlockSpec((1, 1, n), lambda i: (jnp.minimum(i + 2, ntiles - 1), 0, 0)),
                  pl.BlockSpec((tm, d), row), pl.BlockSpec((tm, LANES), row),
                  pl.BlockSpec((1, 8, d), lambda i: (group_of_tile(i), 0, 0)),
                  pl.BlockSpec((1, d), lambda i: (0, 0)),
                  pl.BlockSpec(memory_space=pl.ANY)],
        out_specs=pl.BlockSpec((tm, d), row),
        out_shape=jax.ShapeDtypeStruct((t, d), F32),
        scratch_shapes=[pltpu.SMEM((n,), jnp.int32),
                        pltpu.VMEM((COMBINE_SLOTS, TOP_K, tm * ROW_TILES, LANES), F32),
                        pltpu.SemaphoreType.DMA, pltpu.SemaphoreType.DMA((COMBINE_SLOTS,))],
        compiler_params=_params("arbitrary"),
        name="moe_combine",
    )(dest3, dest3, dest3, x1, route, modtab, g_final.reshape(1, d), yb)


def _grid_swap(x, r1, r2):
    n, d = x.shape
    b = n // (r1 * r2)
    return x.reshape(b, r1, r2, d).swapaxes(1, 2).reshape(n, d)


def _prep_w_in(w_in):
    sizes = (QA_W, QA_W, VA_W, GATE_RANK, VA_W, QKVB_W, 2 * H_B, 2 * H_B, VB_W)
    offs = np.concatenate([[0], np.cumsum(sizes)])
    qa, ka, va, glr, oga, qkvb, ab, bb, zb = [w_in[..., offs[i]:offs[i + 1]] for i in range(len(sizes))]
    pad = jnp.zeros(w_in.shape[:-1] + (SM_W - GATE_RANK - 4 * H_B,), w_in.dtype)
    return jnp.concatenate([qa, ka, va, oga, zb, qkvb, glr, ab, bb, pad], axis=-1).astype(BF16)


def kernel(x_prompt, x_sample, state_gla, state_gdn, c, c_ctx, w_mod, b_mod, g_norm1, g_norm2, w_in,
           gla_w_gate2, gla_b_gate, gla_g_onorm, gdn_conv_w, gdn_a_log, gdn_dt_bias, gdn_g_onorm, w_out,
           router_w, router_b, w_gu, b_gu, w_down, b_down, g_final):
    bp, sp, d = x_prompt.shape
    bs, ns, _ = x_sample.shape
    depth = w_mod.shape[0]
    n_p, n_s = bp * sp, bs * ns
    t = n_p + n_s
    tm = TOKEN_TILE
    nsp, nss = SEQS_PER_STEP_SHORT, SEQS_PER_STEP_LONG
    assert d == D_MODEL and sp % tm == 0 and ns % tm == 0 and bs + 1 <= 8
    assert bp % nsp == 0 and bs % nss == 0 and n_p % (nss * ns) == 0
    rows_s = ns // GRID_W

    def group_of_tile(i):
        return jnp.where(i < n_p // tm, 0, 1 + (i - n_p // tm) // (ns // tm))

    cond = jnp.concatenate([c_ctx[None], c, jnp.zeros((8 - 1 - bs, d), F32)], axis=0)
    mods = _modulation(cond, w_mod, b_mod)
    mods = mods.reshape(depth, 8, N_MOD, d)[:, :1 + bs]
    modtab = jnp.concatenate([mods, jnp.zeros((depth, 1 + bs, 8 - N_MOD, d), F32)], axis=2)

    w_in_r = _prep_w_in(w_in)
    w_out_b = w_out.astype(BF16)
    wg = jnp.concatenate([gla_w_gate2, jnp.zeros((depth, 2, SM_W - GATE_RANK, QA_W), F32)], axis=2)
    wgh = wg.astype(BF16)
    wgl = (wg - wgh.astype(F32)).astype(BF16)
    bg = gla_b_gate.reshape(depth, 2, 1, QA_W)
    lane_pad = lambda v, off: jnp.pad(v.reshape(depth, 1, -1), ((0, 0), (0, 0), (off, SM_W - off - v[0].size)))
    a_log_vec = lane_pad(gdn_a_log, SM_A0)
    dtb_vec = lane_pad(gdn_dt_bias, SM_A0)
    conv_w = jnp.pad(gdn_conv_w, ((0, 0), (0, 8 - gdn_conv_w.shape[1]), (0, 0)))
    gon = jnp.stack([gla_g_onorm, gdn_g_onorm], axis=1)
    rw = jnp.pad(router_w, ((0, 0), (0, 0), (0, LANES - N_EXPERTS)))
    rb = jnp.pad(router_b, ((0, 0), (0, LANES - N_EXPERTS)), constant_values=-1e30).reshape(depth, 1, LANES)
    b_gu_r = jnp.concatenate([b_gu[..., 0::2], b_gu[..., 1::2]], axis=-1)

    hsel = jnp.eye(H_A, dtype=F32)
    st_a_s = jnp.einsum("bldhkv,hg->bldhvgk", state_gla.astype(F32), hsel).reshape(bs, depth, 2, VA_W, QA_W)
    st_b_s = state_gdn.astype(F32)
    st_a_0 = jnp.zeros((nsp, 2, VA_W, QA_W), F32)
    st_b_0 = jnp.zeros((nsp, 2, H_B, DK_B, DV_B), F32)

    n_blocks = (t * TOP_K) // EXPERT_BLOCK + N_EXPERTS
    cap = n_blocks * EXPERT_BLOCK
    tok_of_slot = jnp.repeat(jnp.arange(t, dtype=jnp.int32) * ROW_TILES, TOP_K)

    x = jnp.concatenate([x_prompt.reshape(n_p, d), x_sample.reshape(n_s, d)], axis=0)
    new_gla, new_gdn = [], []
    for l in range(depth):
        if l % 2 == 1:
            x = lax.dynamic_update_slice(x, _grid_swap(x[n_p:], rows_s, GRID_W), (n_p, 0))
        pa, gz, pb, sm = _pre_mixer(x, modtab[l], g_norm1[l], w_in_r[l], group_of_tile)
        oa_p, sa = _gla(pa, sm, st_a_0, wgh[l], wgl[l], bg[l], ntotal=bp, seqlen=sp, nseq=nsp, row_block0=0,
                        shared_state=True)
        oa_s, _ = _gla(pa, sm, st_a_s[:, l], wgh[l], wgl[l], bg[l], ntotal=bs, seqlen=ns, nseq=nss,
                       row_block0=n_p // (nss * ns), shared_state=False)
        ob_p, sb = _gdn(pb, sm, st_b_0, conv_w[l], a_log_vec[l], dtb_vec[l], ntotal=bp, seqlen=sp, nseq=nsp,
                        row_block0=0, shared_state=True)
        ob_s, _ = _gdn(pb, sm, st_b_s[:, l], conv_w[l], a_log_vec[l], dtb_vec[l], ntotal=bs, seqlen=ns, nseq=nss,
                       row_block0=n_p // (nss * ns), shared_state=False)
        new_gla.append(sa)
        new_gdn.append(sb)
        x1, h2t, route, counts = _post_mixer(x, oa_p, oa_s, ob_p, ob_s, gz, modtab[l], gon[l], w_out_b[l],
                                             g_norm2[l], rw[l], rb[l], group_of_tile)

        top_i = route[:, :TOP_K].astype(jnp.int32)
        rank = route[:, 2 * TOP_K:3 * TOP_K].astype(jnp.int32)
        sizes = counts[0, :N_EXPERTS].astype(jnp.int32)
        padded = (sizes + EXPERT_BLOCK - 1) // EXPERT_BLOCK * EXPERT_BLOCK
        pend = jnp.cumsum(padded)
        pstart = pend - padded
        onehot = top_i[..., None] == jnp.arange(N_EXPERTS, dtype=jnp.int32)
        dest = (rank + jnp.sum(jnp.where(onehot, pstart, 0), axis=-1)).reshape(-1)
        dest3 = (dest * ROW_TILES).reshape(t // tm, 1, tm * TOP_K)
        tok3 = jnp.zeros((cap,), jnp.int32).at[dest].set(tok_of_slot, unique_indices=True)
        tok3 = tok3.reshape(n_blocks, 1, EXPERT_BLOCK)
        nblk = (pend[-1] // EXPERT_BLOCK).reshape(1)
        blk_start = jnp.minimum(jnp.arange(n_blocks, dtype=jnp.int32), nblk[0] - 1) * EXPERT_BLOCK
        blk_e = jnp.sum(blk_start[:, None] >= pend[None, :], axis=-1).astype(jnp.int32)

        yb = _experts(l, blk_e, nblk, tok3, h2t, w_gu, b_gu_r[l], w_down, b_down[l])
        x = _combine(dest3, x1, route, modtab[l], g_final, yb, group_of_tile, final=(l == depth - 1))
        if l % 2 == 1:
            x = lax.dynamic_update_slice(x, _grid_swap(x[n_p:], GRID_W, rows_s), (n_p, 0))

    y_prompt = x[:n_p].reshape(bp, sp, d)
    y_sample = x[n_p:].reshape(bs, ns, d)
    sa_all = jnp.stack(new_gla, axis=1).reshape(bp, depth, 2, H_A, DV_A, H_A, DK_A)
    new_state_gla = jnp.einsum("bldhvgk,hg->bldhkv", sa_all, hsel)
    new_state_gdn = jnp.stack(new_gdn, axis=1)
    return (y_prompt, y_sample, new_state_gla, new_state_gdn)
```
